```python
import jax, jax.numpy as jnp
from jax import lax
import numpy as np

D_MODEL = 1024
BATCH = 32
SEQ = 2048
DEPTH = 4

MEM_LEN = 256
POOL_WIDTH = D_MODEL // 4
POOL_GROUPS = 4
POOL_WINDOWS = (2, 4, 8, 16)
HEAD_DIM = 64
ATTN_WIDTH = 3 * D_MODEL // 8
ATTN_HEADS = ATTN_WIDTH // HEAD_DIM
DILATED_PATTERNS = ((128, 1), (512, 4), (2048, 16))
ATTN_BLOCK = 128
LRU_WIDTH = 3 * D_MODEL // 8
LRU_BLOCKS = 6
LRU_CONV = 4
LRU_C = 8.0
MIX_WIDTH = POOL_WIDTH + ATTN_WIDTH + LRU_WIDTH
IN_WIDTH = POOL_WIDTH + 3 * ATTN_WIDTH + 2 * LRU_WIDTH
MEM_HEADS = 4
MEM_HEAD_DIM = D_MODEL // MEM_HEADS
D_FF = 2816
FFN_CONV = 3
EPS = 1e-6

kernel_name = 'hybrid_pool_dilattn_rglru_convffn'


def rmsnorm(x, g):
    xf = x.astype(jnp.float32)
    xf = xf * lax.rsqrt(jnp.mean(xf * xf, axis=-1, keepdims=True) + EPS)
    return (xf * g.astype(jnp.float32)).astype(x.dtype)


def causal_dwconv(x, w, b):
    K = w.shape[0]
    S = x.shape[1]
    xp = jnp.pad(x, ((0, 0), (K - 1, 0), (0, 0)))
    out = b
    for k in range(K):
        out = out + w[k] * xp[:, K - 1 - k:K - 1 - k + S]
    return out


def alibi_slopes(n):
    return jnp.asarray([2.0 ** (-8.0 * (h + 1) / n) for h in range(n)], dtype=jnp.float32)


def pool_mixer(u, pool_w, pool_scale):
    B, S, _ = u.shape
    gw = POOL_WIDTH // POOL_GROUPS
    uf = u.astype(jnp.float32)
    csum = jnp.cumsum(uf, axis=1)
    cp = jnp.concatenate([jnp.zeros_like(csum[:, :1]), csum], axis=1)
    outs = []
    for g, w in enumerate(POOL_WINDOWS):
        sl = slice(g * gw, (g + 1) * gw)
        end = cp[:, 1:, sl]
        start = jnp.concatenate([jnp.zeros((B, w - 1, gw), jnp.float32), cp[:, :S - w + 1, sl]], axis=1)
        count = jnp.minimum(jnp.arange(1, S + 1), w).astype(jnp.float32)[None, :, None]
        outs.append((end - start) / count - uf[:, :, sl])
    pooled = jnp.stack(outs, axis=2)
    mixed = jnp.einsum('bsgc,gcd->bsgd', pooled, pool_w.astype(jnp.float32)).reshape(B, S, POOL_WIDTH)
    return (mixed * pool_scale.astype(jnp.float32)).astype(u.dtype)


def sliding_window_attn(q, k, v, slopes_step, window):
    N, L, H, hd = q.shape
    Q = ATTN_BLOCK
    nb = -(-L // Q)
    Lp = nb * Q
    pad = ((0, 0), (0, Lp - L), (0, 0), (0, 0))
    q, k, v = jnp.pad(q, pad), jnp.pad(k, pad), jnp.pad(v, pad)
    qb = q.reshape(N, nb, Q, H, hd)

    def with_prev(t):
        cur = t.reshape(N, nb, Q, H, hd)
        prev = jnp.concatenate([jnp.zeros_like(cur[:, :1]), cur[:, :-1]], axis=1)
        return jnp.concatenate([prev, cur], axis=2)

    kb, vb = with_prev(k), with_prev(v)
    s = jnp.einsum('nbqhd,nbkhd->nbhqk', qb, kb) * (hd ** -0.5)
    dist = Q + jnp.arange(Q)[:, None] - jnp.arange(2 * Q)[None, :]
    key_pos = jnp.arange(nb)[:, None] * Q - Q + jnp.arange(2 * Q)[None, :]
    valid = ((dist >= 0) & (dist <= window))[None, :, :] & (key_pos >= 0)[:, None, :]
    s = s - slopes_step[:, None, None] * dist.astype(jnp.float32)
    s = jnp.where(valid[None, :, None], s, -jnp.inf)
    m = jnp.max(s, axis=-1, keepdims=True)
    p = jnp.exp(s - m)
    den = jnp.sum(p, axis=-1, keepdims=True)
    o = jnp.einsum('nbhqk,nbkhd->nbqhd', p, vb) / jnp.transpose(den, (0, 1, 3, 2, 4))
    lse = jnp.transpose((m + jnp.log(den))[..., 0], (0, 1, 3, 2))
    return o.reshape(N, Lp, H, hd)[:, :L], lse.reshape(N, Lp, H)[:, :L]


def dilated_attention(q, k, v, slopes):
    B, S, H, hd = q.shape
    outs, lses = [], []
    for window, d in DILATED_PATTERNS:
        L = S // d

        def to_sub(t):
            return t.reshape(B, L, d, H, hd).transpose(0, 2, 1, 3, 4).reshape(B * d, L, H, hd)

        o, lse = sliding_window_attn(to_sub(q), to_sub(k), to_sub(v), slopes * d, window // d)
        outs.append(o.reshape(B, d, L, H, hd).transpose(0, 2, 1, 3, 4).reshape(B, S, H, hd))
        lses.append(lse.reshape(B, d, L, H).transpose(0, 2, 1, 3).reshape(B, S, H))
    wts = jax.nn.softmax(jnp.stack(lses, axis=0), axis=0)
    return jnp.sum(wts[..., None] * jnp.stack(outs, axis=0), axis=0)


def rg_lru(x, w_a, b_a, w_x, b_x, lam):
    B, S, C = x.shape
    xb = x.reshape(B, S, LRU_BLOCKS, C // LRU_BLOCKS)
    r = jax.nn.sigmoid(jnp.einsum('bsgc,gcd->bsgd', xb, w_a.astype(jnp.float32)).reshape(B, S, C) + b_a.astype(jnp.float32))
    i = jax.nn.sigmoid(jnp.einsum('bsgc,gcd->bsgd', xb, w_x.astype(jnp.float32)).reshape(B, S, C) + b_x.astype(jnp.float32))
    log_a = -LRU_C * r * jax.nn.softplus(-lam.astype(jnp.float32))
    a = jnp.exp(log_a)
    bterm = jnp.sqrt(-jnp.expm1(2.0 * log_a)) * (i * x)

    def combine(c1, c2):
        a1, b1 = c1
        a2, b2 = c2
        return a1 * a2, a2 * b1 + b2

    _, h = lax.associative_scan(combine, (a, bterm), axis=1)
    return h


def mixer_block(h, w_in, pool_w, pool_scale, q_gain, k_gain, lru_conv_w, lru_conv_b,
                lru_wa, lru_ba, lru_wx, lru_bx, lru_lambda, w_out, slopes):
    B, S, _ = h.shape
    P, A, R = POOL_WIDTH, ATTN_WIDTH, LRU_WIDTH
    proj = jnp.einsum('bsd,de->bse', h, w_in)
    u_pool, q, k, v, x_lru, y_lru = jnp.split(proj, [P, P + A, P + 2 * A, P + 3 * A, P + 3 * A + R], axis=-1)
    out_pool = pool_mixer(u_pool, pool_w, pool_scale)
    qh = rmsnorm(q.reshape(B, S, ATTN_HEADS, HEAD_DIM), q_gain).astype(jnp.float32)
    kh = rmsnorm(k.reshape(B, S, ATTN_HEADS, HEAD_DIM), k_gain).astype(jnp.float32)
    vh = v.reshape(B, S, ATTN_HEADS, HEAD_DIM).astype(jnp.float32)
    out_attn = dilated_attention(qh, kh, vh, slopes).reshape(B, S, A).astype(h.dtype)
    xc = causal_dwconv(x_lru, lru_conv_w, lru_conv_b).astype(jnp.float32)
    hr = rg_lru(xc, lru_wa, lru_ba, lru_wx, lru_bx, lru_lambda)
    out_lru = (hr * jax.nn.gelu(y_lru.astype(jnp.float32))).astype(h.dtype)
    mixed = jnp.concatenate([out_pool, out_attn, out_lru], axis=-1)
    return jnp.einsum('bse,ed->bsd', mixed, w_out)


def memory_attn(h, mem_n, w_q, w_kv, qg, kg, w_o):
    B, S, _ = h.shape
    M = mem_n.shape[1]
    q = rmsnorm(jnp.einsum('bsd,de->bse', h, w_q).reshape(B, S, MEM_HEADS, MEM_HEAD_DIM), qg)
    k, v = jnp.split(jnp.einsum('bmd,de->bme', mem_n, w_kv), 2, axis=-1)
    k = rmsnorm(k.reshape(B, M, MEM_HEADS, MEM_HEAD_DIM), kg)
    v = v.reshape(B, M, MEM_HEADS, MEM_HEAD_DIM)
    s = jnp.einsum('bshd,bmhd->bhsm', q.astype(jnp.float32), k.astype(jnp.float32)) * (MEM_HEAD_DIM ** -0.5)
    p = jax.nn.softmax(s, axis=-1)
    o = jnp.einsum('bhsm,bmhd->bshd', p, v.astype(jnp.float32)).reshape(B, S, D_MODEL).astype(h.dtype)
    return jnp.einsum('bse,ed->bsd', o, w_o)


def conv_ffn(h, w_up, conv_w, conv_b, w_down):
    g, u = jnp.split(jnp.einsum('bsd,df->bsf', h, w_up), 2, axis=-1)
    g = causal_dwconv(g, conv_w, conv_b)
    return jnp.einsum('bsf,fd->bsd', jax.nn.gelu(g) * u, w_down)


def setup_inputs(seed: int = 0) -> dict:
    key = jax.random.key(seed)
    ks = jax.random.split(key, 32)
    L = DEPTH
    gw = POOL_WIDTH // POOL_GROUPS
    bw = LRU_WIDTH // LRU_BLOCKS

    def nrm(k, shape, scale):
        return jax.random.normal(k, shape, jnp.float32) * scale

    a0 = jax.random.uniform(ks[14], (L, LRU_WIDTH), jnp.float32, 0.9, 0.999)
    return {
        'x': nrm(ks[0], (BATCH, SEQ, D_MODEL), 1.0),
        'mem': nrm(ks[1], (BATCH, MEM_LEN, D_MODEL), 1.0),
        'norm_mix': 1.0 + nrm(ks[2], (L, D_MODEL), 0.1),
        'w_in': nrm(ks[3], (L, D_MODEL, IN_WIDTH), D_MODEL ** -0.5),
        'pool_w': nrm(ks[4], (L, POOL_GROUPS, gw, gw), gw ** -0.5),
        'pool_scale': 1.0 + nrm(ks[5], (L, POOL_WIDTH), 0.1),
        'q_gain': 1.0 + nrm(ks[6], (L, HEAD_DIM), 0.1),
        'k_gain': 1.0 + nrm(ks[7], (L, HEAD_DIM), 0.1),
        'lru_conv_w': nrm(ks[8], (L, LRU_CONV, LRU_WIDTH), LRU_CONV ** -0.5),
        'lru_conv_b': nrm(ks[9], (L, LRU_WIDTH), 0.02),
        'lru_wa': nrm(ks[10], (L, LRU_BLOCKS, bw, bw), bw ** -0.5),
        'lru_ba': nrm(ks[11], (L, LRU_WIDTH), 0.02),
        'lru_wx': nrm(ks[12], (L, LRU_BLOCKS, bw, bw), bw ** -0.5),
        'lru_bx': nrm(ks[13], (L, LRU_WIDTH), 0.02),
        'lru_lambda': jnp.log(a0) - jnp.log1p(-a0),
        'w_out': nrm(ks[15], (L, MIX_WIDTH, D_MODEL), MIX_WIDTH ** -0.5),
        'norm_mem': 1.0 + nrm(ks[16], (L, D_MODEL), 0.1),
        'norm_memkv': 1.0 + nrm(ks[17], (L, D_MODEL), 0.1),
        'w_q_mem': nrm(ks[18], (L, D_MODEL, D_MODEL), D_MODEL ** -0.5),
        'w_kv_mem': nrm(ks[19], (L, D_MODEL, 2 * D_MODEL), D_MODEL ** -0.5),
        'mq_gain': 1.0 + nrm(ks[20], (L, MEM_HEAD_DIM), 0.1),
        'mk_gain': 1.0 + nrm(ks[21], (L, MEM_HEAD_DIM), 0.1),
        'w_o_mem': nrm(ks[22], (L, D_MODEL, D_MODEL), D_MODEL ** -0.5),
        'norm_ffn': 1.0 + nrm(ks[23], (L, D_MODEL), 0.1),
        'w_up': nrm(ks[24], (L, D_MODEL, 2 * D_FF), D_MODEL ** -0.5),
        'ffn_conv_w': nrm(ks[25], (L, FFN_CONV, D_FF), FFN_CONV ** -0.5),
        'ffn_conv_b': nrm(ks[26], (L, D_FF), 0.02),
        'w_down': nrm(ks[27], (L, D_FF, D_MODEL), D_FF ** -0.5),
    }


def reference(x, mem, norm_mix, w_in, pool_w, pool_scale, q_gain, k_gain, lru_conv_w, lru_conv_b,
              lru_wa, lru_ba, lru_wx, lru_bx, lru_lambda, w_out, norm_mem, norm_memkv, w_q_mem,
              w_kv_mem, mq_gain, mk_gain, w_o_mem, norm_ffn, w_up, ffn_conv_w, ffn_conv_b, w_down):
    slopes = alibi_slopes(ATTN_HEADS)
    h = x
    for l in range(DEPTH):
        h = h + mixer_block(rmsnorm(h, norm_mix[l]), w_in[l], pool_w[l], pool_scale[l], q_gain[l], k_gain[l],
                            lru_conv_w[l], lru_conv_b[l], lru_wa[l], lru_ba[l], lru_wx[l], lru_bx[l],
                            lru_lambda[l], w_out[l], slopes)
        h = h + memory_attn(rmsnorm(h, norm_mem[l]), rmsnorm(mem, norm_memkv[l]), w_q_mem[l], w_kv_mem[l],
                            mq_gain[l], mk_gain[l], w_o_mem[l])
        h = h + conv_ffn(rmsnorm(h, norm_ffn[l]), w_up[l], ffn_conv_w[l], ffn_conv_b[l], w_down[l])
    return h
```

```python
import functools

import numpy as np
import jax
import jax.numpy as jnp
from jax import lax
from jax.experimental import pallas as pl
from jax.experimental.pallas import tpu as pltpu

F32 = jnp.float32
BF16 = jnp.bfloat16

LANES = 128
SUBLANES = 8
VMEM_LIMIT_BYTES = 56 * 1024 * 1024

D_MODEL = 1024
POOL_WIDTH = 256
POOL_GROUPS = 4
POOL_WINDOWS = (2, 4, 8, 16)
POOL_HALO = 16
HEAD_DIM = 64
ATTN_WIDTH = 384
ATTN_HEADS = 6
ATTN_PAIRS = ATTN_WIDTH // LANES
DILATIONS = (16, 4, 1)
ATTN_BLOCK = 128
LRU_WIDTH = 384
LRU_BLOCKS = 6
LRU_CONV = 4
LRU_C = 8.0
IN_WIDTH = POOL_WIDTH + 3 * ATTN_WIDTH + 2 * LRU_WIDTH
MEM_HEADS = 4
MEM_HEAD_DIM = D_MODEL // MEM_HEADS
D_FF = 2816
FFN_CONV = 3
FFN_CHUNK = 256
EPS = 1e-6
NEG = -1e30

TM_PROJ = 512
TM_OUT = 512
TM_MEM = 512
TM_FFN = 512

_NT = (((1,), (1,)), ((), ()))


def _rms(x, g):
    ms = jnp.mean(x * x, axis=-1, keepdims=True)
    return x * lax.rsqrt(ms + EPS) * g


def _gelu(x):
    c = 0.7978845608028654
    return x * (0.5 * (1.0 + jnp.tanh(c * (x + 0.044715 * (x * x * x)))))


def _dot(a, b):
    return jnp.dot(a, b, preferred_element_type=F32)


def _proj_kernel(h_ref, gmix_ref, win_ref, ones_ref, qg_ref, kg_ref, poolw_ref, pscale_ref,
                 cw_ref, cb_ref, wg_ref, bg_ref, lam_ref,
                 q_ref, k_ref, v_ref, pool_ref, lru_ref,
                 uext, xext, hcar, hbuf):
    s = pl.program_id(1)
    tm = h_ref.shape[0]

    @pl.when(s == 0)
    def _():
        uext[0:POOL_HALO, :] = jnp.zeros((POOL_HALO, POOL_WIDTH), F32)
        xext[0:SUBLANES, :] = jnp.zeros((SUBLANES, LRU_WIDTH), F32)
        hcar[...] = jnp.zeros(hcar.shape, F32)

    hn = _rms(h_ref[...], gmix_ref[...]).astype(BF16)
    proj = _dot(hn, win_ref[...])
    o = POOL_WIDTH
    u = proj[:, 0:o]
    q = proj[:, o:o + ATTN_WIDTH]
    k = proj[:, o + ATTN_WIDTH:o + 2 * ATTN_WIDTH]
    v = proj[:, o + 2 * ATTN_WIDTH:o + 3 * ATTN_WIDTH]
    x = proj[:, o + 3 * ATTN_WIDTH:o + 3 * ATTN_WIDTH + LRU_WIDTH]
    y = proj[:, o + 3 * ATTN_WIDTH + LRU_WIDTH:]

    ones = ones_ref[...]

    def headnorm(t, gain):
        t2 = t * t
        hi = t2.astype(BF16)
        lo = (t2 - hi.astype(F32)).astype(BF16)
        ssq = _dot(hi, ones) + _dot(lo, ones)
        return t * lax.rsqrt(ssq * (1.0 / HEAD_DIM) + EPS) * gain

    qn = headnorm(q, qg_ref[...])
    kn = headnorm(k, kg_ref[...])
    for c in range(ATTN_PAIRS):
        sl = slice(c * LANES, (c + 1) * LANES)
        q_ref[c] = qn[:, sl]
        k_ref[c] = kn[:, sl]
        v_ref[c] = v[:, sl]

    uext[POOL_HALO:POOL_HALO + tm, :] = u
    e = uext[...]
    s2 = e + pltpu.roll(e, 1, 0)
    s4 = s2 + pltpu.roll(s2, 2, 0)
    s8 = s4 + pltpu.roll(s4, 4, 0)
    s16 = s8 + pltpu.roll(s8, 8, 0)
    uext[0:POOL_HALO, :] = u[tm - POOL_HALO:tm, :]
    tp1 = (s * tm + 1 + lax.broadcasted_iota(jnp.int32, (tm, LANES), 0)).astype(F32)
    low = lax.broadcasted_iota(jnp.int32, (tm, LANES), 1) < POOL_WIDTH // POOL_GROUPS

    def wmean(sw, col, w):
        return sw[POOL_HALO:, col * LANES:(col + 1) * LANES] / jnp.minimum(tp1, float(w))

    pooled = jnp.concatenate(
        [jnp.where(low, wmean(s2, 0, POOL_WINDOWS[0]), wmean(s4, 0, POOL_WINDOWS[1])) - u[:, 0:LANES],
         jnp.where(low, wmean(s8, 1, POOL_WINDOWS[2]), wmean(s16, 1, POOL_WINDOWS[3])) - u[:, LANES:2 * LANES]],
        axis=1)
    pool_ref[...] = (_dot(pooled.astype(BF16), poolw_ref[...]) * pscale_ref[...]).astype(BF16)

    xext[SUBLANES:SUBLANES + tm, :] = x
    ex = xext[...]
    cw = cw_ref[...]
    conv = cb_ref[...] + cw[0:1, :] * ex
    for kk in range(1, LRU_CONV):
        conv = conv + cw[kk:kk + 1, :] * pltpu.roll(ex, kk, 0)
    xc = conv[SUBLANES:, :]
    xext[0:SUBLANES, :] = x[tm - SUBLANES:tm, :]

    gates = _dot(xc.astype(BF16), wg_ref[...]) + bg_ref[...]
    r = jax.nn.sigmoid(gates[:, 0:LRU_WIDTH])
    ig = jax.nn.sigmoid(gates[:, LRU_WIDTH:])
    z = -lam_ref[...]
    softplus = jnp.maximum(z, 0.0) + jnp.log1p(jnp.exp(-jnp.abs(z)))
    log_a = (-LRU_C) * r * softplus
    a = jnp.exp(log_a)
    b = jnp.sqrt(jnp.tanh(-log_a) * (1.0 + a * a)) * (ig * xc)

    sub = lax.broadcasted_iota(jnp.int32, (tm, LRU_WIDTH), 0) & (SUBLANES - 1)
    for sh in (1, 2, 4):
        keep = sub >= sh
        a_s = jnp.where(keep, pltpu.roll(a, sh, 0), 1.0)
        b_s = jnp.where(keep, pltpu.roll(b, sh, 0), 0.0)
        b = a * b_s + b
        a = a * a_s
    hp = hcar[0:1, :]
    for j in range(tm // SUBLANES):
        rows = slice(j * SUBLANES, (j + 1) * SUBLANES)
        hj = a[rows, :] * hp + b[rows, :]
        hbuf[rows, :] = hj
        hp = hj[SUBLANES - 1:SUBLANES, :]
    hcar[...] = jnp.broadcast_to(hp, hcar.shape)
    lru_ref[...] = (hbuf[...] * _gelu(y)).astype(BF16)


def _proj_call(h, l, p):
    B, S, _ = h.shape
    tm = TM_PROJ
    grid = (B, S // tm)
    tile = lambda w: pl.BlockSpec((None, tm, w), lambda b, s: (b, s, 0))
    slab = pl.BlockSpec((None, ATTN_PAIRS, tm, LANES), lambda b, s: (b, 0, s, 0))
    lay = lambda *shape: pl.BlockSpec((None,) + shape, lambda b, s: (l,) + (0,) * len(shape))
    const = lambda *shape: pl.BlockSpec(shape, lambda b, s: (0,) * len(shape))
    slab_shape = jax.ShapeDtypeStruct((B, ATTN_PAIRS, S, LANES), F32)
    return pl.pallas_call(
        _proj_kernel,
        grid=grid,
        in_specs=[tile(D_MODEL), lay(1, D_MODEL), lay(D_MODEL, IN_WIDTH), const(ATTN_WIDTH, ATTN_WIDTH),
                  lay(1, ATTN_WIDTH), lay(1, ATTN_WIDTH), lay(POOL_WIDTH, POOL_WIDTH), lay(1, POOL_WIDTH),
                  lay(LRU_CONV, LRU_WIDTH), lay(1, LRU_WIDTH), lay(LRU_WIDTH, 2 * LRU_WIDTH),
                  lay(1, 2 * LRU_WIDTH), lay(1, LRU_WIDTH)],
        out_specs=[slab, slab, slab, tile(POOL_WIDTH), tile(LRU_WIDTH)],
        out_shape=[slab_shape, slab_shape, slab_shape,
                   jax.ShapeDtypeStruct((B, S, POOL_WIDTH), BF16), jax.ShapeDtypeStruct((B, S, LRU_WIDTH), BF16)],
        scratch_shapes=[pltpu.VMEM((POOL_HALO + tm, POOL_WIDTH), F32), pltpu.VMEM((SUBLANES + tm, LRU_WIDTH), F32),
                        pltpu.VMEM((SUBLANES, LRU_WIDTH), F32), pltpu.VMEM((tm, LRU_WIDTH), F32)],
        compiler_params=pltpu.CompilerParams(dimension_semantics=("arbitrary", "arbitrary"),
                                             vmem_limit_bytes=VMEM_LIMIT_BYTES),
        name="proj",
    )(h, p["norm_mix"], p["w_in"], p["ones"], p["q_gain"], p["k_gain"], p["pool_w"], p["pool_scale"],
      p["lru_conv_w"], p["lru_conv_b"], p["lru_wg"], p["lru_bg"], p["lru_lambda"])


def _attn_kernel(q_ref, k_ref, v_ref, bias_ref, o_ref, qs, ks, vs, oa, la, ob, lb):
    S = o_ref.shape[0]
    nblk = S // ATTN_BLOCK
    Q = ATTN_BLOCK
    low = lax.broadcasted_iota(jnp.int32, (Q, LANES), 1) < HEAD_DIM
    prev_cols = lax.broadcasted_iota(jnp.int32, (2 * Q, 2 * Q), 1) < Q

    ks[0:Q, :] = jnp.zeros((Q, ATTN_WIDTH), BF16)
    vs[0:Q, :] = jnp.zeros((Q, ATTN_WIDTH), BF16)

    def gather(d):
        L = S // d
        for r in range(d):
            for c in range(ATTN_PAIRS):
                cols = slice(c * LANES, (c + 1) * LANES)
                rows = pl.ds(r, L, stride=d) if d > 1 else slice(None)
                qs[r * L:(r + 1) * L, cols] = q_ref[c, rows, :].astype(BF16)
                ks[Q + r * L:Q + (r + 1) * L, cols] = k_ref[c, rows, :].astype(BF16)
                vs[Q + r * L:Q + (r + 1) * L, cols] = v_ref[c, rows, :].astype(BF16)

    def scatter(d_from, d_to, src, dst):
        step = d_from // d_to
        Lf, Lt = S // d_from, S // d_to
        for r in range(d_from):
            start = (r % d_to) * Lt + r // d_to
            for c in range(ATTN_PAIRS):
                dst[c, pl.ds(start, Lf, stride=step), :] = src[c, r * Lf:(r + 1) * Lf, :]

    def process(pidx, d, src, dst):
        nbl = (S // d) // Q

        def body(g, carry):
            r0 = pl.multiple_of(g * Q, Q)
            if nbl > 1:
                pen = jnp.where((g & (nbl - 1)) == 0, NEG, 0.0)
            for c in range(ATTN_PAIRS):
                cols = slice(c * LANES, (c + 1) * LANES)
                qb = qs[pl.ds(r0, Q), cols]
                zero = jnp.zeros_like(qb)
                qq = jnp.concatenate([jnp.where(low, qb, zero), jnp.where(low, zero, qb)], axis=0)
                if nbl > 1:
                    kk = ks[pl.ds(r0, 2 * Q), cols]
                    vv = vs[pl.ds(r0, 2 * Q), cols]
                    sc = lax.dot_general(qq, kk, _NT, preferred_element_type=F32) + bias_ref[pidx, c]
                    sc = sc + jnp.where(prev_cols, pen, 0.0)
                else:
                    kk = ks[pl.ds(r0 + Q, Q), cols]
                    vv = vs[pl.ds(r0 + Q, Q), cols]
                    sc = lax.dot_general(qq, kk, _NT, preferred_element_type=F32) + bias_ref[pidx, c, :, Q:2 * Q]
                m = jnp.max(sc, axis=-1, keepdims=True)
                pr = jnp.exp(sc - m)
                den = jnp.sum(pr, axis=-1, keepdims=True)
                pv = _dot(pr.astype(BF16), vv) / den
                lse = m + jnp.log(den)
                o_new = jnp.where(low, pv[0:Q, :], pv[Q:2 * Q, :])
                l_new = jnp.where(low, lse[0:Q, :], lse[Q:2 * Q, :])
                if src is not None:
                    o_old = src[0][c, pl.ds(r0, Q), :]
                    l_old = src[1][c, pl.ds(r0, Q), :]
                    mx = jnp.maximum(l_new, l_old)
                    wn = jnp.exp(l_new - mx)
                    wo = jnp.exp(l_old - mx)
                    tot = wn + wo
                    o_new = (wn * o_new + wo * o_old) / tot
                    l_new = mx + jnp.log(tot)
                if dst is None:
                    o_ref[pl.ds(r0, Q), cols] = o_new.astype(BF16)
                else:
                    dst[0][c, pl.ds(r0, Q), :] = o_new
                    dst[1][c, pl.ds(r0, Q), :] = l_new
            return carry

        lax.fori_loop(0, nblk, body, 0)

    d0, d1, d2 = DILATIONS
    gather(d0)
    process(0, d0, None, (oa, la))
    scatter(d0, d1, oa, ob)
    scatter(d0, d1, la, lb)
    gather(d1)
    process(1, d1, (ob, lb), (oa, la))
    scatter(d1, d2, oa, ob)
    scatter(d1, d2, la, lb)
    gather(d2)
    process(2, d2, (ob, lb), None)


def _attn_bias():
    Q = ATTN_BLOCK
    dist = (Q + np.arange(Q)[:, None] - np.arange(2 * Q)[None, :])
    valid = (dist >= 0) & (dist <= Q)
    slopes = np.asarray([2.0 ** (-8.0 * (h + 1) / ATTN_HEADS) for h in range(ATTN_HEADS)], dtype=np.float32)
    out = np.empty((len(DILATIONS), ATTN_PAIRS, 2 * Q, 2 * Q), np.float32)
    for pi, d in enumerate(DILATIONS):
        for h in range(ATTN_HEADS):
            step = slopes[h] * np.float32(d)
            tab = np.where(valid, -(step * dist.astype(np.float32)), np.float32(NEG)).astype(np.float32)
            out[pi, h // 2, (h % 2) * Q:(h % 2 + 1) * Q, :] = tab
    return jnp.asarray(out)


def _attn_call(q, k, v, bias):
    B, _, S, _ = q.shape
    slab = pl.BlockSpec((None, ATTN_PAIRS, S, LANES), lambda b: (b, 0, 0, 0))
    acc = pltpu.VMEM((ATTN_PAIRS, S, LANES), F32)
    return pl.pallas_call(
        _attn_kernel,
        grid=(B,),
        in_specs=[slab, slab, slab, pl.BlockSpec(bias.shape, lambda b: (0, 0, 0, 0))],
        out_specs=pl.BlockSpec((None, S, ATTN_WIDTH), lambda b: (b, 0, 0)),
        out_shape=jax.ShapeDtypeStruct((B, S, ATTN_WIDTH), BF16),
        scratch_shapes=[pltpu.VMEM((S, ATTN_WIDTH), BF16), pltpu.VMEM((S + ATTN_BLOCK, ATTN_WIDTH), BF16),
                        pltpu.VMEM((S + ATTN_BLOCK, ATTN_WIDTH), BF16), acc, acc, acc, acc],
        compiler_params=pltpu.CompilerParams(dimension_semantics=("arbitrary",),
                                             vmem_limit_bytes=VMEM_LIMIT_BYTES),
        name="attn",
    )(q, k, v, bias)


def _outproj_kernel(pool_ref, attn_ref, lru_ref, h_ref, w_ref, o_ref, mix):
    mix[:, 0:POOL_WIDTH] = pool_ref[...]
    mix[:, POOL_WIDTH:POOL_WIDTH + ATTN_WIDTH] = attn_ref[...]
    mix[:, POOL_WIDTH + ATTN_WIDTH:] = lru_ref[...]
    o_ref[...] = h_ref[...] + _dot(mix[...], w_ref[...])


def _outproj_call(pool, attn, lru, h, w_out, l):
    B, S, _ = h.shape
    tm = TM_OUT
    tile = lambda w: pl.BlockSpec((None, tm, w), lambda b, s: (b, s, 0))
    return pl.pallas_call(
        _outproj_kernel,
        grid=(B, S // tm),
        in_specs=[tile(POOL_WIDTH), tile(ATTN_WIDTH), tile(LRU_WIDTH), tile(D_MODEL),
                  pl.BlockSpec((None, D_MODEL, D_MODEL), lambda b, s: (l, 0, 0))],
        out_specs=tile(D_MODEL),
        out_shape=jax.ShapeDtypeStruct(h.shape, F32),
        scratch_shapes=[pltpu.VMEM((tm, D_MODEL), BF16)],
        compiler_params=pltpu.CompilerParams(dimension_semantics=("arbitrary", "arbitrary"),
                                             vmem_limit_bytes=VMEM_LIMIT_BYTES),
        name="outproj",
    )(pool, attn, lru, h, w_out)


def _memkv_kernel(mem_ref, g_ref, w_ref, kg_ref, k_ref, v_ref):
    mn = _rms(mem_ref[...], g_ref[...]).astype(BF16)
    kv = _dot(mn, w_ref[...])
    for hh in range(MEM_HEADS):
        cols = slice(hh * MEM_HEAD_DIM, (hh + 1) * MEM_HEAD_DIM)
        k_ref[:, cols] = _rms(kv[:, cols], kg_ref[...]).astype(BF16)
    v_ref[...] = kv[:, D_MODEL:].astype(BF16)


def _memkv_call(mem, p):
    B, M, _ = mem.shape
    L = p["w_kv_mem"].shape[0]
    lay = lambda *shape: pl.BlockSpec((None,) + shape, lambda l, b: (l,) + (0,) * len(shape))
    out = pl.BlockSpec((None, None, M, D_MODEL), lambda l, b: (l, b, 0, 0))
    shape = jax.ShapeDtypeStruct((L, B, M, D_MODEL), BF16)
    return pl.pallas_call(
        _memkv_kernel,
        grid=(L, B),
        in_specs=[pl.BlockSpec((None, M, D_MODEL), lambda l, b: (b, 0, 0)), lay(1, D_MODEL),
                  lay(D_MODEL, 2 * D_MODEL), lay(1, MEM_HEAD_DIM)],
        out_specs=[out, out],
        out_shape=[shape, shape],
        compiler_params=pltpu.CompilerParams(dimension_semantics=("arbitrary", "arbitrary"),
                                             vmem_limit_bytes=VMEM_LIMIT_BYTES),
        name="memkv",
    )(mem, p["norm_memkv"], p["w_kv_mem"], p["mk_gain"])


def _memattn_kernel(h_ref, g_ref, wq_ref, qg_ref, k_ref, v_ref, wo_ref, o_ref, obuf):
    h = h_ref[...]
    q = _dot(_rms(h, g_ref[...]).astype(BF16), wq_ref[...])
    for hh in range(MEM_HEADS):
        cols = slice(hh * MEM_HEAD_DIM, (hh + 1) * MEM_HEAD_DIM)
        qh = _rms(q[:, cols], qg_ref[...]).astype(BF16)
        sc = lax.dot_general(qh, k_ref[:, cols], _NT, preferred_element_type=F32)
        m = jnp.max(sc, axis=-1, keepdims=True)
        pr = jnp.exp(sc - m)
        pr = pr / jnp.sum(pr, axis=-1, keepdims=True)
        obuf[:, cols] = _dot(pr.astype(BF16), v_ref[:, cols]).astype(BF16)
    o_ref[...] = h + _dot(obuf[...], wo_ref[...])


def _memattn_call(h, kmem, vmem, l, p):
    B, S, _ = h.shape
    M = kmem.shape[2]
    tm = TM_MEM
    tile = pl.BlockSpec((None, tm, D_MODEL), lambda b, s: (b, s, 0))
    lay = lambda *shape: pl.BlockSpec((None,) + shape, lambda b, s: (l,) + (0,) * len(shape))
    kv = pl.BlockSpec((None, None, M, D_MODEL), lambda b, s: (l, b, 0, 0))
    return pl.pallas_call(
        _memattn_kernel,
        grid=(B, S // tm),
        in_specs=[tile, lay(1, D_MODEL), lay(D_MODEL, D_MODEL), lay(1, MEM_HEAD_DIM), kv, kv,
                  lay(D_MODEL, D_MODEL)],
        out_specs=tile,
        out_shape=jax.ShapeDtypeStruct(h.shape, F32),
        scratch_shapes=[pltpu.VMEM((tm, D_MODEL), BF16)],
        compiler_params=pltpu.CompilerParams(dimension_semantics=("arbitrary", "arbitrary"),
                                             vmem_limit_bytes=VMEM_LIMIT_BYTES),
        name="memattn",
    )(h, p["norm_mem"], p["w_q_mem"], p["mq_gain"], kmem, vmem, p["w_o_mem"])


def _ffn_kernel(h_ref, g_ref, wup_ref, cw_ref, cb_ref, wdn_ref, o_ref, gcar, act):
    s = pl.program_id(1)
    tm = h_ref.shape[0]

    @pl.when(s == 0)
    def _():
        gcar[...] = jnp.zeros(gcar.shape, F32)

    h = h_ref[...]
    hn = _rms(h, g_ref[...]).astype(BF16)
    for c in range(D_FF // FFN_CHUNK):
        cols = slice(c * FFN_CHUNK, (c + 1) * FFN_CHUNK)
        g = _dot(hn, wup_ref[:, cols])
        u = _dot(hn, wup_ref[:, D_FF + c * FFN_CHUNK:D_FF + (c + 1) * FFN_CHUNK])
        ge = jnp.concatenate([gcar[:, cols], g], axis=0)
        conv = cb_ref[:, cols] + cw_ref[0:1, cols] * ge
        for kk in range(1, FFN_CONV):
            conv = conv + cw_ref[kk:kk + 1, cols] * pltpu.roll(ge, kk, 0)
        gcar[:, cols] = g[tm - SUBLANES:tm, :]
        act[:, cols] = (_gelu(conv[SUBLANES:, :]) * u).astype(BF16)
    o_ref[...] = h + _dot(act[...], wdn_ref[...])


def _ffn_call(h, l, p):
    B, S, _ = h.shape
    tm = TM_FFN
    tile = pl.BlockSpec((None, tm, D_MODEL), lambda b, s: (b, s, 0))
    lay = lambda *shape: pl.BlockSpec((None,) + shape, lambda b, s: (l,) + (0,) * len(shape))
    return pl.pallas_call(
        _ffn_kernel,
        grid=(B, S // tm),
        in_specs=[tile, lay(1, D_MODEL), lay(D_MODEL, 2 * D_FF), lay(FFN_CONV, D_FF), lay(1, D_FF),
                  lay(D_FF, D_MODEL)],
        out_specs=tile,
        out_shape=jax.ShapeDtypeStruct(h.shape, F32),
        scratch_shapes=[pltpu.VMEM((SUBLANES, D_FF), F32), pltpu.VMEM((tm, D_FF), BF16)],
        compiler_params=pltpu.CompilerParams(dimension_semantics=("arbitrary", "arbitrary"),
                                             vmem_limit_bytes=VMEM_LIMIT_BYTES),
        name="ffn",
    )(h, p["norm_ffn"], p["w_up"], p["ffn_conv_w"], p["ffn_conv_b"], p["w_down"])


def _block_diag(w):
    L, G, n, _ = w.shape
    eye = jnp.eye(G, dtype=w.dtype)
    return jnp.einsum("lgij,gh->lgihj", w, eye).reshape(L, G * n, G * n)


def kernel(x, mem, norm_mix, w_in, pool_w, pool_scale, q_gain, k_gain, lru_conv_w, lru_conv_b, lru_wa, lru_ba,
           lru_wx, lru_bx, lru_lambda, w_out, norm_mem, norm_memkv, w_q_mem, w_kv_mem, mq_gain, mk_gain, w_o_mem,
           norm_ffn, w_up, ffn_conv_w, ffn_conv_b, w_down):
    depth = w_in.shape[0]
    row = lambda a: a.astype(F32)[:, None, :]
    p = {
        "norm_mix": row(norm_mix),
        "w_in": w_in.astype(BF16),
        "ones": _block_diag(jnp.ones((1, ATTN_HEADS, HEAD_DIM, HEAD_DIM), BF16))[0],
        "q_gain": row(jnp.tile(q_gain, (1, ATTN_HEADS)) * (HEAD_DIM ** -0.5)),
        "k_gain": row(jnp.tile(k_gain, (1, ATTN_HEADS))),
        "pool_w": _block_diag(pool_w).astype(BF16),
        "pool_scale": row(pool_scale),
        "lru_conv_w": lru_conv_w.astype(F32),
        "lru_conv_b": row(lru_conv_b),
        "lru_wg": jnp.concatenate([_block_diag(lru_wa), _block_diag(lru_wx)], axis=-1).astype(BF16),
        "lru_bg": row(jnp.concatenate([lru_ba, lru_bx], axis=-1)),
        "lru_lambda": row(lru_lambda),
        "norm_mem": row(norm_mem),
        "norm_memkv": row(norm_memkv),
        "w_q_mem": w_q_mem.astype(BF16),
        "w_kv_mem": w_kv_mem.astype(BF16),
        "mq_gain": row(mq_gain * (MEM_HEAD_DIM ** -0.5)),
        "mk_gain": row(mk_gain),
        "w_o_mem": w_o_mem.astype(BF16),
        "norm_ffn": row(norm_ffn),
        "w_up": w_up.astype(BF16),
        "ffn_conv_w": ffn_conv_w.astype(F32),
        "ffn_conv_b": row(ffn_conv_b),
        "w_down": w_down.astype(BF16),
    }
    w_out_b = w_out.astype(BF16)
    bias = _attn_bias()
    kmem, vmem = _memkv_call(mem, p)
    h = x
    for l in range(depth):
        q, k, v, pool, lru = _proj_call(h, l, p)
        attn = _attn_call(q, k, v, bias)
        h = _outproj_call(pool, attn, lru, h, w_out_b, l)
        h = _memattn_call(h, kmem, vmem, l, p)
        h = _ffn_call(h, l, p)
    return h
```

```python
import numpy as np
import jax
import jax.numpy as jnp
from jax import lax
from jax.experimental import pallas as pl
from jax.experimental.pallas import tpu as pltpu

F32 = jnp.float32
BF16 = jnp.bfloat16

LANES = 128
SUBLANES = 8
VMEM_LIMIT_BYTES = 56 * 1024 * 1024

D_MODEL = 1024
POOL_WIDTH = 256
POOL_GROUPS = 4
POOL_WINDOWS = (2, 4, 8, 16)
POOL_HALO = 16
HEAD_DIM = 64
ATTN_WIDTH = 384
ATTN_HEADS = 6
ATTN_PAIRS = ATTN_WIDTH // LANES
DILATIONS = (16, 4, 1)
ATTN_BLOCK = 128
ATTN_UNROLL = 4
LRU_WIDTH = 384
LRU_BLOCKS = 6
LRU_CONV = 4
LRU_C = 8.0
IN_WIDTH = POOL_WIDTH + 3 * ATTN_WIDTH + 2 * LRU_WIDTH
MEM_HEADS = 4
MEM_HEAD_DIM = D_MODEL // MEM_HEADS
D_FF = 2816
FFN_CONV = 3
FFN_CHUNK = 256
EPS = 1e-6
NEG = -1e30
LOG2E = 1.4426950408889634

TM_PROJ = 512
TM_MEM = 512
TM_FFN = 512

_NT = (((1,), (1,)), ((), ()))


def _rms(x, g):
    ms = jnp.mean(x * x, axis=-1, keepdims=True)
    return x * lax.rsqrt(ms + EPS) * g


def _gelu(x):
    c = 0.7978845608028654
    return x * (0.5 * (1.0 + jnp.tanh(c * (x + 0.044715 * (x * x * x)))))


def _dot(a, b):
    return jnp.dot(a, b, preferred_element_type=F32)


def _proj_kernel(h_ref, gmix_ref, win_ref, qg_ref, kg_ref, poolw_ref, pscale_ref,
                 cw_ref, cb_ref, wg_ref, bg_ref, lam_ref,
                 q_ref, k_ref, v_ref, pool_ref, lru_ref,
                 uext, xext, hcar, hbuf):
    s = pl.program_id(1)
    tm = h_ref.shape[0]

    @pl.when(s == 0)
    def _():
        uext[0:POOL_HALO, :] = jnp.zeros((POOL_HALO, POOL_WIDTH), F32)
        xext[0:SUBLANES, :] = jnp.zeros((SUBLANES, LRU_WIDTH), F32)
        hcar[...] = jnp.zeros(hcar.shape, F32)

    hn = _rms(h_ref[...], gmix_ref[...]).astype(BF16)
    proj = _dot(hn, win_ref[...])
    o = POOL_WIDTH
    u = proj[:, 0:o]
    q = proj[:, o:o + ATTN_WIDTH]
    k = proj[:, o + ATTN_WIDTH:o + 2 * ATTN_WIDTH]
    v = proj[:, o + 2 * ATTN_WIDTH:o + 3 * ATTN_WIDTH]
    x = proj[:, o + 3 * ATTN_WIDTH:o + 3 * ATTN_WIDTH + LRU_WIDTH]
    y = proj[:, o + 3 * ATTN_WIDTH + LRU_WIDTH:]

    low_head = lax.broadcasted_iota(jnp.int32, (tm, LANES), 1) < HEAD_DIM

    def headnorm(t, gain):
        t2 = t * t
        parts = []
        for c in range(ATTN_PAIRS):
            blk = t2[:, c * LANES:(c + 1) * LANES]
            even = jnp.sum(jnp.where(low_head, blk, 0.0), axis=-1, keepdims=True)
            odd = jnp.sum(jnp.where(low_head, 0.0, blk), axis=-1, keepdims=True)
            parts.append(jnp.where(low_head, even, odd))
        ssq = jnp.concatenate(parts, axis=1)
        return t * lax.rsqrt(ssq * (1.0 / HEAD_DIM) + EPS) * gain

    qn = headnorm(q, qg_ref[...])
    kn = headnorm(k, kg_ref[...])
    for c in range(ATTN_PAIRS):
        sl = slice(c * LANES, (c + 1) * LANES)
        q_ref[c] = qn[:, sl]
        k_ref[c] = kn[:, sl]
        v_ref[c] = v[:, sl]

    uext[POOL_HALO:POOL_HALO + tm, :] = u
    e = uext[...]
    s2 = e + pltpu.roll(e, 1, 0)
    s4 = s2 + pltpu.roll(s2, 2, 0)
    s8 = s4 + pltpu.roll(s4, 4, 0)
    s16 = s8 + pltpu.roll(s8, 8, 0)
    uext[0:POOL_HALO, :] = u[tm - POOL_HALO:tm, :]
    tp1 = (s * tm + 1 + lax.broadcasted_iota(jnp.int32, (tm, LANES), 0)).astype(F32)
    low = lax.broadcasted_iota(jnp.int32, (tm, LANES), 1) < POOL_WIDTH // POOL_GROUPS

    def wmean(sw, col, w):
        return sw[POOL_HALO:, col * LANES:(col + 1) * LANES] / jnp.minimum(tp1, float(w))

    pooled = jnp.concatenate(
        [jnp.where(low, wmean(s2, 0, POOL_WINDOWS[0]), wmean(s4, 0, POOL_WINDOWS[1])) - u[:, 0:LANES],
         jnp.where(low, wmean(s8, 1, POOL_WINDOWS[2]), wmean(s16, 1, POOL_WINDOWS[3])) - u[:, LANES:2 * LANES]],
        axis=1)
    pool_ref[...] = (_dot(pooled.astype(BF16), poolw_ref[...]) * pscale_ref[...]).astype(BF16)

    xext[SUBLANES:SUBLANES + tm, :] = x
    ex = xext[...]
    cw = cw_ref[...]
    conv = cb_ref[...] + cw[0:1, :] * ex
    for kk in range(1, LRU_CONV):
        conv = conv + cw[kk:kk + 1, :] * pltpu.roll(ex, kk, 0)
    xc = conv[SUBLANES:, :]
    xext[0:SUBLANES, :] = x[tm - SUBLANES:tm, :]

    gates = _dot(xc.astype(BF16), wg_ref[...]) + bg_ref[...]
    r = jax.nn.sigmoid(gates[:, 0:LRU_WIDTH])
    ig = jax.nn.sigmoid(gates[:, LRU_WIDTH:])
    z = -lam_ref[...]
    softplus = jnp.maximum(z, 0.0) + jnp.log1p(jnp.exp(-jnp.abs(z)))
    log_a = (-LRU_C) * r * softplus
    a = jnp.exp(log_a)
    b = jnp.sqrt(jnp.tanh(-log_a) * (1.0 + a * a)) * (ig * xc)

    sub = lax.broadcasted_iota(jnp.int32, (tm, LRU_WIDTH), 0) & (SUBLANES - 1)
    for sh in (1, 2, 4):
        keep = sub >= sh
        a_s = jnp.where(keep, pltpu.roll(a, sh, 0), 1.0)
        b_s = jnp.where(keep, pltpu.roll(b, sh, 0), 0.0)
        b = a * b_s + b
        a = a * a_s
    hp = hcar[0:1, :]
    for j in range(tm // SUBLANES):
        rows = slice(j * SUBLANES, (j + 1) * SUBLANES)
        hj = a[rows, :] * hp + b[rows, :]
        hbuf[rows, :] = hj
        hp = hj[SUBLANES - 1:SUBLANES, :]
    hcar[...] = jnp.broadcast_to(hp, hcar.shape)
    lru_ref[...] = (hbuf[...] * _gelu(y)).astype(BF16)


def _proj_call(h, l, p):
    B, S, _ = h.shape
    tm = TM_PROJ
    grid = (B, S // tm)
    tile = lambda w: pl.BlockSpec((None, tm, w), lambda b, s: (b, s, 0))
    slab = pl.BlockSpec((None, ATTN_PAIRS, tm, LANES), lambda b, s: (b, 0, s, 0))
    lay = lambda *shape: pl.BlockSpec((None,) + shape, lambda b, s: (l,) + (0,) * len(shape))
    slab_shape = jax.ShapeDtypeStruct((B, ATTN_PAIRS, S, LANES), F32)
    return pl.pallas_call(
        _proj_kernel,
        grid=grid,
        in_specs=[tile(D_MODEL), lay(1, D_MODEL), lay(D_MODEL, IN_WIDTH),
                  lay(1, ATTN_WIDTH), lay(1, ATTN_WIDTH), lay(POOL_WIDTH, POOL_WIDTH), lay(1, POOL_WIDTH),
                  lay(LRU_CONV, LRU_WIDTH), lay(1, LRU_WIDTH), lay(LRU_WIDTH, 2 * LRU_WIDTH),
                  lay(1, 2 * LRU_WIDTH), lay(1, LRU_WIDTH)],
        out_specs=[slab, slab, slab, tile(POOL_WIDTH), tile(LRU_WIDTH)],
        out_shape=[slab_shape, slab_shape, slab_shape,
                   jax.ShapeDtypeStruct((B, S, POOL_WIDTH), BF16), jax.ShapeDtypeStruct((B, S, LRU_WIDTH), BF16)],
        scratch_shapes=[pltpu.VMEM((POOL_HALO + tm, POOL_WIDTH), F32), pltpu.VMEM((SUBLANES + tm, LRU_WIDTH), F32),
                        pltpu.VMEM((SUBLANES, LRU_WIDTH), F32), pltpu.VMEM((tm, LRU_WIDTH), F32)],
        compiler_params=pltpu.CompilerParams(dimension_semantics=("arbitrary", "arbitrary"),
                                             vmem_limit_bytes=VMEM_LIMIT_BYTES),
        name="proj",
    )(h, p["norm_mix"], p["w_in"], p["q_gain"], p["k_gain"], p["pool_w"], p["pool_scale"],
      p["lru_conv_w"], p["lru_conv_b"], p["lru_wg"], p["lru_bg"], p["lru_lambda"])


def _attn_kernel(q_ref, k_ref, v_ref, bias_ref, o_ref, qs, ks, vs, oa, ma, la, ob, mb, lb):
    S = o_ref.shape[0]
    nblk = S // ATTN_BLOCK
    Q = ATTN_BLOCK
    low = lax.broadcasted_iota(jnp.int32, (Q, LANES), 1) < HEAD_DIM

    ks[0:Q, :] = jnp.zeros((Q, ATTN_WIDTH), BF16)
    vs[0:Q, :] = jnp.zeros((Q, ATTN_WIDTH), BF16)

    def gather(d):
        L = S // d
        for r in range(d):
            for c in range(ATTN_PAIRS):
                cols = slice(c * LANES, (c + 1) * LANES)
                rows = pl.ds(r, L, stride=d) if d > 1 else slice(None)
                qs[r * L:(r + 1) * L, cols] = q_ref[c, rows, :].astype(BF16)
                ks[Q + r * L:Q + (r + 1) * L, cols] = k_ref[c, rows, :].astype(BF16)
                vs[Q + r * L:Q + (r + 1) * L, cols] = v_ref[c, rows, :].astype(BF16)

    def scatter(d_from, d_to, src, dst):
        step = d_from // d_to
        Lf, Lt = S // d_from, S // d_to
        for r in range(d_from):
            start = (r % d_to) * Lt + r // d_to
            for c in range(ATTN_PAIRS):
                dst[c, pl.ds(start, Lf, stride=step), :] = src[c, r * Lf:(r + 1) * Lf, :]

    def process(pidx, d, src, dst):
        nbl = (S // d) // Q

        def body(g, carry):
            r0 = pl.multiple_of(g * Q, Q)
            if nbl > 1:
                first = jnp.where((g & (nbl - 1)) == 0, 1, 0)
            for c in range(ATTN_PAIRS):
                cols = slice(c * LANES, (c + 1) * LANES)
                qb = qs[pl.ds(r0, Q), cols]
                zero = jnp.zeros_like(qb)
                qq = jnp.concatenate([jnp.where(low, qb, zero), jnp.where(low, zero, qb)], axis=0)
                if nbl > 1:
                    kk = ks[pl.ds(r0, 2 * Q), cols]
                    vv = vs[pl.ds(r0, 2 * Q), cols]
                    sc = lax.dot_general(qq, kk, _NT, preferred_element_type=F32) + bias_ref[pidx, first, c]
                else:
                    kk = ks[pl.ds(r0 + Q, Q), cols]
                    vv = vs[pl.ds(r0 + Q, Q), cols]
                    sc = lax.dot_general(qq, kk, _NT, preferred_element_type=F32) + bias_ref[pidx, 1, c, :, Q:2 * Q]
                m = jnp.max(sc, axis=-1, keepdims=True)
                pr = jnp.exp2(sc - m)
                den = jnp.sum(pr, axis=-1, keepdims=True)
                pv = _dot(pr.astype(BF16), vv)
                o_new = jnp.where(low, pv[0:Q, :], pv[Q:2 * Q, :])
                m_new = jnp.where(low, m[0:Q, :], m[Q:2 * Q, :])
                l_new = jnp.where(low, den[0:Q, :], den[Q:2 * Q, :])
                if src is not None:
                    m_old = src[1][c, pl.ds(r0, Q), :]
                    mx = jnp.maximum(m_new, m_old)
                    wn = jnp.exp2(m_new - mx)
                    wo = jnp.exp2(m_old - mx)
                    o_new = wn * o_new + wo * src[0][c, pl.ds(r0, Q), :]
                    l_new = wn * l_new + wo * src[2][c, pl.ds(r0, Q), :]
                    m_new = mx
                if dst is None:
                    o_ref[pl.ds(r0, Q), cols] = (o_new / l_new).astype(BF16)
                else:
                    dst[0][c, pl.ds(r0, Q), :] = o_new
                    dst[1][c, pl.ds(r0, Q), :] = m_new
                    dst[2][c, pl.ds(r0, Q), :] = l_new
            return carry

        lax.fori_loop(0, nblk, body, 0, unroll=ATTN_UNROLL)

    d0, d1, d2 = DILATIONS
    acc_a, acc_b = (oa, ma, la), (ob, mb, lb)
    gather(d0)
    process(0, d0, None, acc_a)
    for sa, sb in zip(acc_a, acc_b):
        scatter(d0, d1, sa, sb)
    gather(d1)
    process(1, d1, acc_b, acc_a)
    for sa, sb in zip(acc_a, acc_b):
        scatter(d1, d2, sa, sb)
    gather(d2)
    process(2, d2, acc_b, None)


def _attn_bias():
    Q = ATTN_BLOCK
    dist = (Q + np.arange(Q)[:, None] - np.arange(2 * Q)[None, :])
    valid = (dist >= 0) & (dist <= Q)
    has_prev = np.arange(2 * Q)[None, :] >= Q
    slopes = np.asarray([2.0 ** (-8.0 * (h + 1) / ATTN_HEADS) for h in range(ATTN_HEADS)], dtype=np.float32)
    out = np.empty((len(DILATIONS), 2, ATTN_PAIRS, 2 * Q, 2 * Q), np.float32)
    for pi, d in enumerate(DILATIONS):
        for h in range(ATTN_HEADS):
            step = slopes[h] * np.float32(d)
            pen = (step * dist.astype(np.float32)).astype(np.float64) * LOG2E
            rows = slice((h % 2) * Q, (h % 2 + 1) * Q)
            out[pi, 0, h // 2, rows, :] = np.where(valid, -pen, NEG)
            out[pi, 1, h // 2, rows, :] = np.where(valid & has_prev, -pen, NEG)
    return jnp.asarray(out)


def _attn_call(q, k, v, bias):
    B, _, S, _ = q.shape
    slab = pl.BlockSpec((None, ATTN_PAIRS, S, LANES), lambda b: (b, 0, 0, 0))
    acc = pltpu.VMEM((ATTN_PAIRS, S, LANES), F32)
    return pl.pallas_call(
        _attn_kernel,
        grid=(B,),
        in_specs=[slab, slab, slab,
                  pl.BlockSpec(bias.shape, lambda b: (0, 0, 0, 0, 0), pipeline_mode=pl.Buffered(1))],
        out_specs=pl.BlockSpec((None, S, ATTN_WIDTH), lambda b: (b, 0, 0)),
        out_shape=jax.ShapeDtypeStruct((B, S, ATTN_WIDTH), BF16),
        scratch_shapes=[pltpu.VMEM((S, ATTN_WIDTH), BF16), pltpu.VMEM((S + ATTN_BLOCK, ATTN_WIDTH), BF16),
                        pltpu.VMEM((S + ATTN_BLOCK, ATTN_WIDTH), BF16), acc, acc, acc, acc, acc, acc],
        compiler_params=pltpu.CompilerParams(dimension_semantics=("arbitrary",),
                                             vmem_limit_bytes=VMEM_LIMIT_BYTES),
        name="attn",
    )(q, k, v, bias)


def _memkv_kernel(mem_ref, g_ref, w_ref, kg_ref, k_ref, v_ref):
    mn = _rms(mem_ref[...], g_ref[...]).astype(BF16)
    kv = _dot(mn, w_ref[...])
    for hh in range(MEM_HEADS):
        cols = slice(hh * MEM_HEAD_DIM, (hh + 1) * MEM_HEAD_DIM)
        k_ref[:, cols] = _rms(kv[:, cols], kg_ref[...]).astype(BF16)
    v_ref[...] = kv[:, D_MODEL:].astype(BF16)


def _memkv_call(mem, p):
    B, M, _ = mem.shape
    L = p["w_kv_mem"].shape[0]
    lay = lambda *shape: pl.BlockSpec((None,) + shape, lambda l, b: (l,) + (0,) * len(shape))
    out = pl.BlockSpec((None, None, M, D_MODEL), lambda l, b: (l, b, 0, 0))
    shape = jax.ShapeDtypeStruct((L, B, M, D_MODEL), BF16)
    return pl.pallas_call(
        _memkv_kernel,
        grid=(L, B),
        in_specs=[pl.BlockSpec((None, M, D_MODEL), lambda l, b: (b, 0, 0)), lay(1, D_MODEL),
                  lay(D_MODEL, 2 * D_MODEL), lay(1, MEM_HEAD_DIM)],
        out_specs=[out, out],
        out_shape=[shape, shape],
        compiler_params=pltpu.CompilerParams(dimension_semantics=("arbitrary", "arbitrary"),
                                             vmem_limit_bytes=VMEM_LIMIT_BYTES),
        name="memkv",
    )(mem, p["norm_memkv"], p["w_kv_mem"], p["mk_gain"])


def _mixmem_kernel(pool_ref, attn_ref, lru_ref, h_ref, wout_ref, g_ref, wq_ref, qg_ref, k_ref, v_ref, wo_ref,
                   o_ref, mix, obuf):
    mix[:, 0:POOL_WIDTH] = pool_ref[...]
    mix[:, POOL_WIDTH:POOL_WIDTH + ATTN_WIDTH] = attn_ref[...]
    mix[:, POOL_WIDTH + ATTN_WIDTH:] = lru_ref[...]
    h = h_ref[...] + _dot(mix[...], wout_ref[...])
    q = _dot(_rms(h, g_ref[...]).astype(BF16), wq_ref[...])
    for hh in range(MEM_HEADS):
        cols = slice(hh * MEM_HEAD_DIM, (hh + 1) * MEM_HEAD_DIM)
        qh = _rms(q[:, cols], qg_ref[...]).astype(BF16)
        sc = lax.dot_general(qh, k_ref[:, cols], _NT, preferred_element_type=F32)
        m = jnp.max(sc, axis=-1, keepdims=True)
        pr = jnp.exp(sc - m)
        pr = pr / jnp.sum(pr, axis=-1, keepdims=True)
        obuf[:, cols] = _dot(pr.astype(BF16), v_ref[:, cols]).astype(BF16)
    o_ref[...] = h + _dot(obuf[...], wo_ref[...])


def _mixmem_call(pool, attn, lru, h, kmem, vmem, l, p):
    B, S, _ = h.shape
    M = kmem.shape[2]
    tm = TM_MEM
    tile = lambda w: pl.BlockSpec((None, tm, w), lambda b, s: (b, s, 0))
    lay = lambda *shape: pl.BlockSpec((None,) + shape, lambda b, s: (l,) + (0,) * len(shape))
    kv = pl.BlockSpec((None, None, M, D_MODEL), lambda b, s: (l, b, 0, 0))
    return pl.pallas_call(
        _mixmem_kernel,
        grid=(B, S // tm),
        in_specs=[tile(POOL_WIDTH), tile(ATTN_WIDTH), tile(LRU_WIDTH), tile(D_MODEL), lay(D_MODEL, D_MODEL),
                  lay(1, D_MODEL), lay(D_MODEL, D_MODEL), lay(1, MEM_HEAD_DIM), kv, kv, lay(D_MODEL, D_MODEL)],
        out_specs=tile(D_MODEL),
        out_shape=jax.ShapeDtypeStruct(h.shape, F32),
        scratch_shapes=[pltpu.VMEM((tm, D_MODEL), BF16), pltpu.VMEM((tm, D_MODEL), BF16)],
        compiler_params=pltpu.CompilerParams(dimension_semantics=("arbitrary", "arbitrary"),
                                             vmem_limit_bytes=VMEM_LIMIT_BYTES),
        name="mixmem",
    )(pool, attn, lru, h, p["w_out"], p["norm_mem"], p["w_q_mem"], p["mq_gain"], kmem, vmem, p["w_o_mem"])


def _ffn_kernel(h_ref, g_ref, wup_ref, cw_ref, cb_ref, wdn_ref, o_ref, gcar, act):
    s = pl.program_id(1)
    tm = h_ref.shape[0]

    @pl.when(s == 0)
    def _():
        gcar[...] = jnp.zeros(gcar.shape, F32)

    h = h_ref[...]
    hn = _rms(h, g_ref[...]).astype(BF16)
    for c in range(D_FF // FFN_CHUNK):
        cols = slice(c * FFN_CHUNK, (c + 1) * FFN_CHUNK)
        g = _dot(hn, wup_ref[:, cols])
        u = _dot(hn, wup_ref[:, D_FF + c * FFN_CHUNK:D_FF + (c + 1) * FFN_CHUNK])
        ge = jnp.concatenate([gcar[:, cols], g], axis=0)
        conv = cb_ref[:, cols] + cw_ref[0:1, cols] * ge
        for kk in range(1, FFN_CONV):
            conv = conv + cw_ref[kk:kk + 1, cols] * pltpu.roll(ge, kk, 0)
        gcar[:, cols] = g[tm - SUBLANES:tm, :]
        act[:, cols] = (_gelu(conv[SUBLANES:, :]) * u).astype(BF16)
    o_ref[...] = h + _dot(act[...], wdn_ref[...])


def _ffn_call(h, l, p):
    B, S, _ = h.shape
    tm = TM_FFN
    tile = pl.BlockSpec((None, tm, D_MODEL), lambda b, s: (b, s, 0))
    lay = lambda *shape: pl.BlockSpec((None,) + shape, lambda b, s: (l,) + (0,) * len(shape))
    return pl.pallas_call(
        _ffn_kernel,
        grid=(B, S // tm),
        in_specs=[tile, lay(1, D_MODEL), lay(D_MODEL, 2 * D_FF), lay(FFN_CONV, D_FF), lay(1, D_FF),
                  lay(D_FF, D_MODEL)],
        out_specs=tile,
        out_shape=jax.ShapeDtypeStruct(h.shape, F32),
        scratch_shapes=[pltpu.VMEM((SUBLANES, D_FF), F32), pltpu.VMEM((tm, D_FF), BF16)],
        compiler_params=pltpu.CompilerParams(dimension_semantics=("arbitrary", "arbitrary"),
                                             vmem_limit_bytes=VMEM_LIMIT_BYTES),
        name="ffn",
    )(h, p["norm_ffn"], p["w_up"], p["ffn_conv_w"], p["ffn_conv_b"], p["w_down"])


def _block_diag(w):
    L, G, n, _ = w.shape
    eye = jnp.eye(G, dtype=w.dtype)
    return jnp.einsum("lgij,gh->lgihj", w, eye).reshape(L, G * n, G * n)


def kernel(x, mem, norm_mix, w_in, pool_w, pool_scale, q_gain, k_gain, lru_conv_w, lru_conv_b, lru_wa, lru_ba,
           lru_wx, lru_bx, lru_lambda, w_out, norm_mem, norm_memkv, w_q_mem, w_kv_mem, mq_gain, mk_gain, w_o_mem,
           norm_ffn, w_up, ffn_conv_w, ffn_conv_b, w_down):
    depth = w_in.shape[0]
    row = lambda a: a.astype(F32)[:, None, :]
    p = {
        "norm_mix": row(norm_mix),
        "w_in": w_in.astype(BF16),
        "q_gain": row(jnp.tile(q_gain, (1, ATTN_HEADS)) * (HEAD_DIM ** -0.5 * LOG2E)),
        "k_gain": row(jnp.tile(k_gain, (1, ATTN_HEADS))),
        "pool_w": _block_diag(pool_w).astype(BF16),
        "pool_scale": row(pool_scale),
        "lru_conv_w": lru_conv_w.astype(F32),
        "lru_conv_b": row(lru_conv_b),
        "lru_wg": jnp.concatenate([_block_diag(lru_wa), _block_diag(lru_wx)], axis=-1).astype(BF16),
        "lru_bg": row(jnp.concatenate([lru_ba, lru_bx], axis=-1)),
        "lru_lambda": row(lru_lambda),
        "w_out": w_out.astype(BF16),
        "norm_mem": row(norm_mem),
        "norm_memkv": row(norm_memkv),
        "w_q_mem": w_q_mem.astype(BF16),
        "w_kv_mem": w_kv_mem.astype(BF16),
        "mq_gain": row(mq_gain * (MEM_HEAD_DIM ** -0.5)),
        "mk_gain": row(mk_gain),
        "w_o_mem": w_o_mem.astype(BF16),
        "norm_ffn": row(norm_ffn),
        "w_up": w_up.astype(BF16),
        "ffn_conv_w": ffn_conv_w.astype(F32),
        "ffn_conv_b": row(ffn_conv_b),
        "w_down": w_down.astype(BF16),
    }
    bias = _attn_bias()
    kmem, vmem = _memkv_call(mem, p)
    h = x
    for l in range(depth):
        q, k, v, pool, lru = _proj_call(h, l, p)
        attn = _attn_call(q, k, v, bias)
        h = _mixmem_call(pool, attn, lru, h, kmem, vmem, l, p)
        h = _ffn_call(h, l, p)
    return h
```

```python
import numpy as np
import jax
import jax.numpy as jnp
from jax import lax
from jax.experimental import pallas as pl
from jax.experimental.pallas import tpu as pltpu

F32 = jnp.float32
BF16 = jnp.bfloat16

LANES = 128
SUBLANES = 8
VMEM_LIMIT_BYTES = 56 * 1024 * 1024

D_MODEL = 1024
POOL_WIDTH = 256
POOL_GROUPS = 4
POOL_WINDOWS = (2, 4, 8, 16)
POOL_HALO = 16
HEAD_DIM = 64
ATTN_WIDTH = 384
ATTN_HEADS = 6
ATTN_PAIRS = ATTN_WIDTH // LANES
DILATIONS = (16, 4, 1)
ATTN_BLOCK = 128
ATTN_UNROLL = 4
LRU_WIDTH = 384
LRU_BLOCKS = 6
LRU_CONV = 4
LRU_C = 8.0
IN_WIDTH = POOL_WIDTH + 3 * ATTN_WIDTH + 2 * LRU_WIDTH
MEM_HEADS = 4
MEM_HEAD_DIM = D_MODEL // MEM_HEADS
D_FF = 2816
FFN_CONV = 3
FFN_CHUNK = 256
EPS = 1e-6
NEG = -1e30
LOG2E = 1.4426950408889634

TM_PROJ = 512
TM_MEM = 512
TM_FFN = 512

_NT = (((1,), (1,)), ((), ()))


def _rms(x, g):
    ms = jnp.mean(x * x, axis=-1, keepdims=True)
    return x * lax.rsqrt(ms + EPS) * g


def _gelu(x):
    c = 0.7978845608028654
    return x * (0.5 * (1.0 + jnp.tanh(c * (x + 0.044715 * (x * x * x)))))


def _dot(a, b):
    return jnp.dot(a, b, preferred_element_type=F32)


def _proj_kernel(h_ref, gmix_ref, win_ref, qg_ref, kg_ref, poolw_ref, pscale_ref,
                 cw_ref, cb_ref, wg_ref, bg_ref, lam_ref,
                 q_ref, k_ref, v_ref, pool_ref, lru_ref,
                 uext, xext, hcar, hbuf):
    s = pl.program_id(1)
    tm = h_ref.shape[0]

    @pl.when(s == 0)
    def _():
        uext[0:POOL_HALO, :] = jnp.zeros((POOL_HALO, POOL_WIDTH), F32)
        xext[0:SUBLANES, :] = jnp.zeros((SUBLANES, LRU_WIDTH), F32)
        hcar[...] = jnp.zeros(hcar.shape, F32)

    hn = _rms(h_ref[...], gmix_ref[...]).astype(BF16)
    o = POOL_WIDTH
    xy = _dot(hn, win_ref[:, o + 3 * ATTN_WIDTH:])
    x = xy[:, 0:LRU_WIDTH]
    y = xy[:, LRU_WIDTH:]

    xext[SUBLANES:SUBLANES + tm, :] = x
    ex = xext[...]
    cw = cw_ref[...]
    conv = cb_ref[...] + cw[0:1, :] * ex
    for kk in range(1, LRU_CONV):
        conv = conv + cw[kk:kk + 1, :] * pltpu.roll(ex, kk, 0)
    xc = conv[SUBLANES:, :]
    xext[0:SUBLANES, :] = x[tm - SUBLANES:tm, :]

    gates = _dot(xc.astype(BF16), wg_ref[...]) + bg_ref[...]
    r = jax.nn.sigmoid(gates[:, 0:LRU_WIDTH])
    ig = jax.nn.sigmoid(gates[:, LRU_WIDTH:])
    z = -lam_ref[...]
    softplus = jnp.maximum(z, 0.0) + jnp.log1p(jnp.exp(-jnp.abs(z)))
    log_a = (-LRU_C) * r * softplus
    a = jnp.exp(log_a)
    b = jnp.sqrt(jnp.tanh(-log_a) * (1.0 + a * a)) * (ig * xc)

    sub = lax.broadcasted_iota(jnp.int32, (tm, LRU_WIDTH), 0) & (SUBLANES - 1)
    for sh in (1, 2, 4):
        inside = sub >= sh
        a_s = jnp.where(inside, pltpu.roll(a, sh, 0), 1.0)
        b_s = jnp.where(inside, pltpu.roll(b, sh, 0), 0.0)
        b = a * b_s + b
        a = a * a_s
    hp = hcar[0:1, :]
    for j in range(tm // SUBLANES):
        rows = slice(j * SUBLANES, (j + 1) * SUBLANES)
        hj = a[rows, :] * hp + b[rows, :]
        hbuf[rows, :] = hj
        hp = hj[SUBLANES - 1:SUBLANES, :]
    hcar[...] = jnp.broadcast_to(hp, hcar.shape)
    lru_ref[...] = (hbuf[...] * _gelu(y)).astype(BF16)

    qk = _dot(hn, win_ref[:, o:o + 2 * ATTN_WIDTH])
    low_head = lax.broadcasted_iota(jnp.int32, (tm, LANES), 1) < HEAD_DIM

    def headnorm(t, gain):
        t2 = t * t
        parts = []
        for c in range(ATTN_PAIRS):
            blk = t2[:, c * LANES:(c + 1) * LANES]
            even = jnp.sum(jnp.where(low_head, blk, 0.0), axis=-1, keepdims=True)
            odd = jnp.sum(jnp.where(low_head, 0.0, blk), axis=-1, keepdims=True)
            parts.append(jnp.where(low_head, even, odd))
        ssq = jnp.concatenate(parts, axis=1)
        return t * lax.rsqrt(ssq * (1.0 / HEAD_DIM) + EPS) * gain

    qn = headnorm(qk[:, 0:ATTN_WIDTH], qg_ref[...])
    kn = headnorm(qk[:, ATTN_WIDTH:], kg_ref[...])
    for c in range(ATTN_PAIRS):
        sl = slice(c * LANES, (c + 1) * LANES)
        q_ref[c] = qn[:, sl]
        k_ref[c] = kn[:, sl]

    u = _dot(hn, win_ref[:, 0:o])
    uext[POOL_HALO:POOL_HALO + tm, :] = u
    e = uext[...]
    s2 = e + pltpu.roll(e, 1, 0)
    s4 = s2 + pltpu.roll(s2, 2, 0)
    s8 = s4 + pltpu.roll(s4, 4, 0)
    s16 = s8 + pltpu.roll(s8, 8, 0)
    uext[0:POOL_HALO, :] = u[tm - POOL_HALO:tm, :]
    tp1 = (s * tm + 1 + lax.broadcasted_iota(jnp.int32, (tm, LANES), 0)).astype(F32)
    low = lax.broadcasted_iota(jnp.int32, (tm, LANES), 1) < POOL_WIDTH // POOL_GROUPS

    def wmean(sw, col, w):
        return sw[POOL_HALO:, col * LANES:(col + 1) * LANES] / jnp.minimum(tp1, float(w))

    pooled = jnp.concatenate(
        [jnp.where(low, wmean(s2, 0, POOL_WINDOWS[0]), wmean(s4, 0, POOL_WINDOWS[1])) - u[:, 0:LANES],
         jnp.where(low, wmean(s8, 1, POOL_WINDOWS[2]), wmean(s16, 1, POOL_WINDOWS[3])) - u[:, LANES:2 * LANES]],
        axis=1)
    pool_ref[...] = (_dot(pooled.astype(BF16), poolw_ref[...]) * pscale_ref[...]).astype(BF16)

    v = _dot(hn, win_ref[:, o + 2 * ATTN_WIDTH:o + 3 * ATTN_WIDTH])
    for c in range(ATTN_PAIRS):
        v_ref[c] = v[:, c * LANES:(c + 1) * LANES]


def _proj_call(h, l, p):
    B, S, _ = h.shape
    tm = TM_PROJ
    grid = (B, S // tm)
    tile = lambda w: pl.BlockSpec((None, tm, w), lambda b, s: (b, s, 0))
    slab = pl.BlockSpec((None, ATTN_PAIRS, tm, LANES), lambda b, s: (b, 0, s, 0))
    lay = lambda *shape: pl.BlockSpec((None,) + shape, lambda b, s: (l,) + (0,) * len(shape))
    slab_shape = jax.ShapeDtypeStruct((B, ATTN_PAIRS, S, LANES), F32)
    return pl.pallas_call(
        _proj_kernel,
        grid=grid,
        in_specs=[tile(D_MODEL), lay(1, D_MODEL), lay(D_MODEL, IN_WIDTH),
                  lay(1, ATTN_WIDTH), lay(1, ATTN_WIDTH), lay(POOL_WIDTH, POOL_WIDTH), lay(1, POOL_WIDTH),
                  lay(LRU_CONV, LRU_WIDTH), lay(1, LRU_WIDTH), lay(LRU_WIDTH, 2 * LRU_WIDTH),
                  lay(1, 2 * LRU_WIDTH), lay(1, LRU_WIDTH)],
        out_specs=[slab, slab, slab, tile(POOL_WIDTH), tile(LRU_WIDTH)],
        out_shape=[slab_shape, slab_shape, slab_shape,
                   jax.ShapeDtypeStruct((B, S, POOL_WIDTH), BF16), jax.ShapeDtypeStruct((B, S, LRU_WIDTH), BF16)],
        scratch_shapes=[pltpu.VMEM((POOL_HALO + tm, POOL_WIDTH), F32), pltpu.VMEM((SUBLANES + tm, LRU_WIDTH), F32),
                        pltpu.VMEM((SUBLANES, LRU_WIDTH), F32), pltpu.VMEM((tm, LRU_WIDTH), F32)],
        compiler_params=pltpu.CompilerParams(dimension_semantics=("arbitrary", "arbitrary"),
                                             vmem_limit_bytes=VMEM_LIMIT_BYTES),
        name="proj",
    )(h, p["norm_mix"], p["w_in"], p["q_gain"], p["k_gain"], p["pool_w"], p["pool_scale"],
      p["lru_conv_w"], p["lru_conv_b"], p["lru_wg"], p["lru_bg"], p["lru_lambda"])


def _attn_kernel(q_ref, k_ref, v_ref, bias_ref, o_ref, qs, ks, vs, oa, ma, la, ob, mb, lb):
    S = o_ref.shape[0]
    nblk = S // ATTN_BLOCK
    Q = ATTN_BLOCK
    low = lax.broadcasted_iota(jnp.int32, (Q, LANES), 1) < HEAD_DIM

    ks[0:Q, :] = jnp.zeros((Q, ATTN_WIDTH), BF16)
    vs[0:Q, :] = jnp.zeros((Q, ATTN_WIDTH), BF16)

    def regroup(src, d_from, step, store):
        Lf, Ln = S // d_from, S // (d_from * step)
        for rf in range(d_from):
            for rs in range(step):
                r_new = rs * d_from + rf
                for c in range(ATTN_PAIRS):
                    rows = pl.ds(rf * Lf + rs, Ln, stride=step) if step > 1 else slice(rf * Lf, (rf + 1) * Lf)
                    store(c, r_new * Ln, Ln, src[c, rows, :])

    def load_qkv(srcs, d_from, step):
        for src, dst, off in zip(srcs, (qs, ks, vs), (0, Q, Q)):
            def store(c, row, n, val, dst=dst, off=off):
                dst[off + row:off + row + n, c * LANES:(c + 1) * LANES] = val.astype(BF16)
            regroup(src, d_from, step, store)

    def reorder_f32(src, dst, d_from, step):
        def store(c, row, n, val):
            dst[c, row:row + n, :] = val
        regroup(src, d_from, step, store)

    def scatter(d_from, d_to, src, dst):
        step = d_from // d_to
        Lf, Lt = S // d_from, S // d_to
        for r in range(d_from):
            start = (r % d_to) * Lt + r // d_to
            for c in range(ATTN_PAIRS):
                dst[c, pl.ds(start, Lf, stride=step), :] = src[c, r * Lf:(r + 1) * Lf, :]

    def process(pidx, d, src, dst):
        nbl = (S // d) // Q

        def body(g, carry):
            r0 = pl.multiple_of(g * Q, Q)
            if nbl > 1:
                first = jnp.where((g & (nbl - 1)) == 0, 1, 0)
            for c in range(ATTN_PAIRS):
                cols = slice(c * LANES, (c + 1) * LANES)
                qb = qs[pl.ds(r0, Q), cols]
                zero = jnp.zeros_like(qb)
                qq = jnp.concatenate([jnp.where(low, qb, zero), jnp.where(low, zero, qb)], axis=0)
                if nbl > 1:
                    kk = ks[pl.ds(r0, 2 * Q), cols]
                    vv = vs[pl.ds(r0, 2 * Q), cols]
                    sc = lax.dot_general(qq, kk, _NT, preferred_element_type=F32) + bias_ref[pidx, first, c]
                else:
                    kk = ks[pl.ds(r0 + Q, Q), cols]
                    vv = vs[pl.ds(r0 + Q, Q), cols]
                    sc = lax.dot_general(qq, kk, _NT, preferred_element_type=F32) + bias_ref[pidx, 1, c, :, Q:2 * Q]
                m = jnp.max(sc, axis=-1, keepdims=True)
                pr = jnp.exp2(sc - m)
                den = jnp.sum(pr, axis=-1, keepdims=True)
                pv = _dot(pr.astype(BF16), vv)
                o_new = jnp.where(low, pv[0:Q, :], pv[Q:2 * Q, :])
                m_new = jnp.where(low, m[0:Q, :], m[Q:2 * Q, :])
                l_new = jnp.where(low, den[0:Q, :], den[Q:2 * Q, :])
                if src is not None:
                    m_old = src[1][c, pl.ds(r0, Q), :]
                    mx = jnp.maximum(m_new, m_old)
                    wn = jnp.exp2(m_new - mx)
                    wo = jnp.exp2(m_old - mx)
                    o_new = wn * o_new + wo * src[0][c, pl.ds(r0, Q), :]
                    l_new = wn * l_new + wo * src[2][c, pl.ds(r0, Q), :]
                    m_new = mx
                if dst is None:
                    o_ref[pl.ds(r0, Q), cols] = (o_new / l_new).astype(BF16)
                else:
                    dst[0][c, pl.ds(r0, Q), :] = o_new
                    dst[1][c, pl.ds(r0, Q), :] = m_new
                    dst[2][c, pl.ds(r0, Q), :] = l_new
            return carry

        lax.fori_loop(0, nblk, body, 0, unroll=ATTN_UNROLL)

    d0, d1, d2 = DILATIONS
    acc_a, acc_b = (oa, ma, la), (ob, mb, lb)
    for src, dst in zip((q_ref, k_ref, v_ref), acc_b):
        reorder_f32(src, dst, d2, d1 // d2)
    load_qkv(acc_b, d1, d0 // d1)
    process(0, d0, None, acc_a)
    load_qkv(acc_b, d1, 1)
    for sa, sb in zip(acc_a, acc_b):
        scatter(d0, d1, sa, sb)
    process(1, d1, acc_b, acc_a)
    for sa, sb in zip(acc_a, acc_b):
        scatter(d1, d2, sa, sb)
    load_qkv((q_ref, k_ref, v_ref), d2, 1)
    process(2, d2, acc_b, None)


def _attn_bias():
    Q = ATTN_BLOCK
    dist = (Q + np.arange(Q)[:, None] - np.arange(2 * Q)[None, :])
    valid = (dist >= 0) & (dist <= Q)
    has_prev = np.arange(2 * Q)[None, :] >= Q
    slopes = np.asarray([2.0 ** (-8.0 * (h + 1) / ATTN_HEADS) for h in range(ATTN_HEADS)], dtype=np.float32)
    out = np.empty((len(DILATIONS), 2, ATTN_PAIRS, 2 * Q, 2 * Q), np.float32)
    for pi, d in enumerate(DILATIONS):
        for h in range(ATTN_HEADS):
            step = slopes[h] * np.float32(d)
            pen = (step * dist.astype(np.float32)).astype(np.float64) * LOG2E
            rows = slice((h % 2) * Q, (h % 2 + 1) * Q)
            out[pi, 0, h // 2, rows, :] = np.where(valid, -pen, NEG)
            out[pi, 1, h // 2, rows, :] = np.where(valid & has_prev, -pen, NEG)
    return jnp.asarray(out)


def _attn_call(q, k, v, bias):
    B, _, S, _ = q.shape
    slab = pl.BlockSpec((None, ATTN_PAIRS, S, LANES), lambda b: (b, 0, 0, 0))
    acc = pltpu.VMEM((ATTN_PAIRS, S, LANES), F32)
    return pl.pallas_call(
        _attn_kernel,
        grid=(B,),
        in_specs=[slab, slab, slab,
                  pl.BlockSpec(bias.shape, lambda b: (0, 0, 0, 0, 0), pipeline_mode=pl.Buffered(1))],
        out_specs=pl.BlockSpec((None, S, ATTN_WIDTH), lambda b: (b, 0, 0)),
        out_shape=jax.ShapeDtypeStruct((B, S, ATTN_WIDTH), BF16),
        scratch_shapes=[pltpu.VMEM((S, ATTN_WIDTH), BF16), pltpu.VMEM((S + ATTN_BLOCK, ATTN_WIDTH), BF16),
                        pltpu.VMEM((S + ATTN_BLOCK, ATTN_WIDTH), BF16), acc, acc, acc, acc, acc, acc],
        compiler_params=pltpu.CompilerParams(dimension_semantics=("arbitrary",),
                                             vmem_limit_bytes=VMEM_LIMIT_BYTES),
        name="attn",
    )(q, k, v, bias)


def _memkv_kernel(mem_ref, g_ref, w_ref, kg_ref, k_ref, v_ref):
    mn = _rms(mem_ref[...], g_ref[...]).astype(BF16)
    kv = _dot(mn, w_ref[...])
    for hh in range(MEM_HEADS):
        cols = slice(hh * MEM_HEAD_DIM, (hh + 1) * MEM_HEAD_DIM)
        k_ref[:, cols] = _rms(kv[:, cols], kg_ref[...]).astype(BF16)
    v_ref[...] = kv[:, D_MODEL:].astype(BF16)


def _memkv_call(mem, p):
    B, M, _ = mem.shape
    L = p["w_kv_mem"].shape[0]
    lay = lambda *shape: pl.BlockSpec((None,) + shape, lambda l, b: (l,) + (0,) * len(shape))
    out = pl.BlockSpec((None, None, M, D_MODEL), lambda l, b: (l, b, 0, 0))
    shape = jax.ShapeDtypeStruct((L, B, M, D_MODEL), BF16)
    return pl.pallas_call(
        _memkv_kernel,
        grid=(L, B),
        in_specs=[pl.BlockSpec((None, M, D_MODEL), lambda l, b: (b, 0, 0)), lay(1, D_MODEL),
                  lay(D_MODEL, 2 * D_MODEL), lay(1, MEM_HEAD_DIM)],
        out_specs=[out, out],
        out_shape=[shape, shape],
        compiler_params=pltpu.CompilerParams(dimension_semantics=("arbitrary", "arbitrary"),
                                             vmem_limit_bytes=VMEM_LIMIT_BYTES),
        name="memkv",
    )(mem, p["norm_memkv"], p["w_kv_mem"], p["mk_gain"])


def _mixmem_kernel(pool_ref, attn_ref, lru_ref, h_ref, wout_ref, g_ref, wq_ref, qg_ref, k_ref, v_ref, wo_ref,
                   o_ref, mix, obuf):
    mix[:, 0:POOL_WIDTH] = pool_ref[...]
    mix[:, POOL_WIDTH:POOL_WIDTH + ATTN_WIDTH] = attn_ref[...]
    mix[:, POOL_WIDTH + ATTN_WIDTH:] = lru_ref[...]
    h = h_ref[...] + _dot(mix[...], wout_ref[...])
    q = _dot(_rms(h, g_ref[...]).astype(BF16), wq_ref[...])
    for hh in range(MEM_HEADS):
        cols = slice(hh * MEM_HEAD_DIM, (hh + 1) * MEM_HEAD_DIM)
        qh = _rms(q[:, cols], qg_ref[...]).astype(BF16)
        sc = lax.dot_general(qh, k_ref[:, cols], _NT, preferred_element_type=F32)
        m = jnp.max(sc, axis=-1, keepdims=True)
        pr = jnp.exp(sc - m)
        pr = pr / jnp.sum(pr, axis=-1, keepdims=True)
        obuf[:, cols] = _dot(pr.astype(BF16), v_ref[:, cols]).astype(BF16)
    o_ref[...] = h + _dot(obuf[...], wo_ref[...])


def _mixmem_call(pool, attn, lru, h, kmem, vmem, l, p):
    B, S, _ = h.shape
    M = kmem.shape[2]
    tm = TM_MEM
    tile = lambda w: pl.BlockSpec((None, tm, w), lambda b, s: (b, s, 0))
    lay = lambda *shape: pl.BlockSpec((None,) + shape, lambda b, s: (l,) + (0,) * len(shape))
    kv = pl.BlockSpec((None, None, M, D_MODEL), lambda b, s: (l, b, 0, 0))
    return pl.pallas_call(
        _mixmem_kernel,
        grid=(B, S // tm),
        in_specs=[tile(POOL_WIDTH), tile(ATTN_WIDTH), tile(LRU_WIDTH), tile(D_MODEL), lay(D_MODEL, D_MODEL),
                  lay(1, D_MODEL), lay(D_MODEL, D_MODEL), lay(1, MEM_HEAD_DIM), kv, kv, lay(D_MODEL, D_MODEL)],
        out_specs=tile(D_MODEL),
        out_shape=jax.ShapeDtypeStruct(h.shape, F32),
        scratch_shapes=[pltpu.VMEM((tm, D_MODEL), BF16), pltpu.VMEM((tm, D_MODEL), BF16)],
        compiler_params=pltpu.CompilerParams(dimension_semantics=("arbitrary", "arbitrary"),
                                             vmem_limit_bytes=VMEM_LIMIT_BYTES),
        name="mixmem",
    )(pool, attn, lru, h, p["w_out"], p["norm_mem"], p["w_q_mem"], p["mq_gain"], kmem, vmem, p["w_o_mem"])


def _ffn_kernel(h_ref, g_ref, wup_ref, cw_ref, cb_ref, wdn_ref, o_ref, gcar, act):
    s = pl.program_id(1)
    tm = h_ref.shape[0]

    @pl.when(s == 0)
    def _():
        gcar[...] = jnp.zeros(gcar.shape, F32)

    h = h_ref[...]
    hn = _rms(h, g_ref[...]).astype(BF16)
    for c in range(D_FF // FFN_CHUNK):
        cols = slice(c * FFN_CHUNK, (c + 1) * FFN_CHUNK)
        g = _dot(hn, wup_ref[:, cols])
        u = _dot(hn, wup_ref[:, D_FF + c * FFN_CHUNK:D_FF + (c + 1) * FFN_CHUNK])
        ge = jnp.concatenate([gcar[:, cols], g], axis=0)
        conv = cb_ref[:, cols] + cw_ref[0:1, cols] * ge
        for kk in range(1, FFN_CONV):
            conv = conv + cw_ref[kk:kk + 1, cols] * pltpu.roll(ge, kk, 0)
        gcar[:, cols] = g[tm - SUBLANES:tm, :]
        act[:, cols] = (_gelu(conv[SUBLANES:, :]) * u).astype(BF16)
    o_ref[...] = h + _dot(act[...], wdn_ref[...])


def _ffn_call(h, l, p):
    B, S, _ = h.shape
    tm = TM_FFN
    tile = pl.BlockSpec((None, tm, D_MODEL), lambda b, s: (b, s, 0))
    lay = lambda *shape: pl.BlockSpec((None,) + shape, lambda b, s: (l,) + (0,) * len(shape))
    return pl.pallas_call(
        _ffn_kernel,
        grid=(B, S // tm),
        in_specs=[tile, lay(1, D_MODEL), lay(D_MODEL, 2 * D_FF), lay(FFN_CONV, D_FF), lay(1, D_FF),
                  lay(D_FF, D_MODEL)],
        out_specs=tile,
        out_shape=jax.ShapeDtypeStruct(h.shape, F32),
        scratch_shapes=[pltpu.VMEM((SUBLANES, D_FF), F32), pltpu.VMEM((tm, D_FF), BF16)],
        compiler_params=pltpu.CompilerParams(dimension_semantics=("arbitrary", "arbitrary"),
                                             vmem_limit_bytes=VMEM_LIMIT_BYTES),
        name="ffn",
    )(h, p["norm_ffn"], p["w_up"], p["ffn_conv_w"], p["ffn_conv_b"], p["w_down"])


def _block_diag(w):
    L, G, n, _ = w.shape
    eye = jnp.eye(G, dtype=w.dtype)
    return jnp.einsum("lgij,gh->lgihj", w, eye).reshape(L, G * n, G * n)


def kernel(x, mem, norm_mix, w_in, pool_w, pool_scale, q_gain, k_gain, lru_conv_w, lru_conv_b, lru_wa, lru_ba,
           lru_wx, lru_bx, lru_lambda, w_out, norm_mem, norm_memkv, w_q_mem, w_kv_mem, mq_gain, mk_gain, w_o_mem,
           norm_ffn, w_up, ffn_conv_w, ffn_conv_b, w_down):
    depth = w_in.shape[0]
    row = lambda a: a.astype(F32)[:, None, :]
    p = {
        "norm_mix": row(norm_mix),
        "w_in": w_in.astype(BF16),
        "q_gain": row(jnp.tile(q_gain, (1, ATTN_HEADS)) * (HEAD_DIM ** -0.5 * LOG2E)),
        "k_gain": row(jnp.tile(k_gain, (1, ATTN_HEADS))),
        "pool_w": _block_diag(pool_w).astype(BF16),
        "pool_scale": row(pool_scale),
        "lru_conv_w": lru_conv_w.astype(F32),
        "lru_conv_b": row(lru_conv_b),
        "lru_wg": jnp.concatenate([_block_diag(lru_wa), _block_diag(lru_wx)], axis=-1).astype(BF16),
        "lru_bg": row(jnp.concatenate([lru_ba, lru_bx], axis=-1)),
        "lru_lambda": row(lru_lambda),
        "w_out": w_out.astype(BF16),
        "norm_mem": row(norm_mem),
        "norm_memkv": row(norm_memkv),
        "w_q_mem": w_q_mem.astype(BF16),
        "w_kv_mem": w_kv_mem.astype(BF16),
        "mq_gain": row(mq_gain * (MEM_HEAD_DIM ** -0.5)),
        "mk_gain": row(mk_gain),
        "w_o_mem": w_o_mem.astype(BF16),
        "norm_ffn": row(norm_ffn),
        "w_up": w_up.astype(BF16),
        "ffn_conv_w": ffn_conv_w.astype(F32),
        "ffn_conv_b": row(ffn_conv_b),
        "w_down": w_down.astype(BF16),
    }
    bias = _attn_bias()
    kmem, vmem = _memkv_call(mem, p)
    h = x
    for l in range(depth):
        q, k, v, pool, lru = _proj_call(h, l, p)
        attn = _attn_call(q, k, v, bias)
        h = _mixmem_call(pool, attn, lru, h, kmem, vmem, l, p)
        h = _ffn_call(h, l, p)
    return h
```

```python
import numpy as np
import jax
import jax.numpy as jnp
from jax import lax
from jax.experimental import pallas as pl
from jax.experimental.pallas import tpu as pltpu

F32 = jnp.float32
BF16 = jnp.bfloat16

LANES = 128
SUBLANES = 8
VMEM_LIMIT_BYTES = 56 * 1024 * 1024

D_MODEL = 1024
POOL_WIDTH = 256
POOL_GROUPS = 4
POOL_WINDOWS = (2, 4, 8, 16)
POOL_HALO = 16
HEAD_DIM = 64
ATTN_WIDTH = 384
ATTN_HEADS = 6
ATTN_PAIRS = ATTN_WIDTH // LANES
DILATIONS = (16, 4, 1)
ATTN_BLOCK = 128
ATTN_UNROLL = 8
LRU_WIDTH = 384
LRU_BLOCKS = 6
LRU_CONV = 4
LRU_C = 8.0
IN_WIDTH = POOL_WIDTH + 3 * ATTN_WIDTH + 2 * LRU_WIDTH
MEM_HEADS = 4
MEM_HEAD_DIM = D_MODEL // MEM_HEADS
D_FF = 2816
FFN_CONV = 3
FFN_CHUNK = 256
EPS = 1e-6
NEG = -1e30
LOG2E = 1.4426950408889634

TM_PROJ = 512
TM_MEM = 1024
TM_FFN = 1024

_NT = (((1,), (1,)), ((), ()))


def _rms(x, g):
    ms = jnp.mean(x * x, axis=-1, keepdims=True)
    return x * lax.rsqrt(ms + EPS) * g


def _gelu(x):
    c = 0.7978845608028654
    return x * (0.5 * (1.0 + jnp.tanh(c * (x + 0.044715 * (x * x * x)))))


def _sigmoid(x):
    return 0.5 * jnp.tanh(0.5 * x) + 0.5


def _dot(a, b):
    return jnp.dot(a, b, preferred_element_type=F32)


def _proj_kernel(h_ref, gmix_ref, win_ref, qg_ref, kg_ref, poolw_ref, pscale_ref,
                 cw_ref, cb_ref, wg_ref, bg_ref, lam_ref,
                 q_ref, k_ref, v_ref, pool_ref, lru_ref,
                 uext, xext, hcar, hbuf):
    s = pl.program_id(1)
    tm = h_ref.shape[0]

    @pl.when(s == 0)
    def _():
        uext[0:POOL_HALO, :] = jnp.zeros((POOL_HALO, POOL_WIDTH), F32)
        xext[0:SUBLANES, :] = jnp.zeros((SUBLANES, LRU_WIDTH), F32)
        hcar[...] = jnp.zeros(hcar.shape, F32)

    hn = _rms(h_ref[...], gmix_ref[...]).astype(BF16)
    proj = _dot(hn, win_ref[...])
    o = POOL_WIDTH
    u = proj[:, 0:o]
    q = proj[:, o:o + ATTN_WIDTH]
    k = proj[:, o + ATTN_WIDTH:o + 2 * ATTN_WIDTH]
    v = proj[:, o + 2 * ATTN_WIDTH:o + 3 * ATTN_WIDTH]
    x = proj[:, o + 3 * ATTN_WIDTH:o + 3 * ATTN_WIDTH + LRU_WIDTH]
    y = proj[:, o + 3 * ATTN_WIDTH + LRU_WIDTH:]

    low_head = lax.broadcasted_iota(jnp.int32, (tm, LANES), 1) < HEAD_DIM

    def headnorm(t, gain):
        t2 = t * t
        parts = []
        for c in range(ATTN_PAIRS):
            blk = t2[:, c * LANES:(c + 1) * LANES]
            even = jnp.sum(jnp.where(low_head, blk, 0.0), axis=-1, keepdims=True)
            odd = jnp.sum(jnp.where(low_head, 0.0, blk), axis=-1, keepdims=True)
            parts.append(jnp.where(low_head, even, odd))
        ssq = jnp.concatenate(parts, axis=1)
        return t * lax.rsqrt(ssq * (1.0 / HEAD_DIM) + EPS) * gain

    qn = headnorm(q, qg_ref[...])
    kn = headnorm(k, kg_ref[...])
    for c in range(ATTN_PAIRS):
        sl = slice(c * LANES, (c + 1) * LANES)
        q_ref[c] = qn[:, sl]
        k_ref[c] = kn[:, sl]
        v_ref[c] = v[:, sl]

    uext[POOL_HALO:POOL_HALO + tm, :] = u
    e = uext[...]
    s2 = e + pltpu.roll(e, 1, 0)
    s4 = s2 + pltpu.roll(s2, 2, 0)
    s8 = s4 + pltpu.roll(s4, 4, 0)
    s16 = s8 + pltpu.roll(s8, 8, 0)
    uext[0:POOL_HALO, :] = u[tm - POOL_HALO:tm, :]
    tp1 = (s * tm + 1 + lax.broadcasted_iota(jnp.int32, (tm, LANES), 0)).astype(F32)
    low = lax.broadcasted_iota(jnp.int32, (tm, LANES), 1) < POOL_WIDTH // POOL_GROUPS

    def wmean(sw, col, w):
        return sw[POOL_HALO:, col * LANES:(col + 1) * LANES] / jnp.minimum(tp1, float(w))

    pooled = jnp.concatenate(
        [jnp.where(low, wmean(s2, 0, POOL_WINDOWS[0]), wmean(s4, 0, POOL_WINDOWS[1])) - u[:, 0:LANES],
         jnp.where(low, wmean(s8, 1, POOL_WINDOWS[2]), wmean(s16, 1, POOL_WINDOWS[3])) - u[:, LANES:2 * LANES]],
        axis=1)
    pool_ref[...] = (_dot(pooled.astype(BF16), poolw_ref[...]) * pscale_ref[...]).astype(BF16)

    xext[SUBLANES:SUBLANES + tm, :] = x
    ex = xext[...]
    cw = cw_ref[...]
    conv = cb_ref[...] + cw[0:1, :] * ex
    for kk in range(1, LRU_CONV):
        conv = conv + cw[kk:kk + 1, :] * pltpu.roll(ex, kk, 0)
    xc = conv[SUBLANES:, :]
    xext[0:SUBLANES, :] = x[tm - SUBLANES:tm, :]

    gates = _dot(xc.astype(BF16), wg_ref[...]) + bg_ref[...]
    r = _sigmoid(gates[:, 0:LRU_WIDTH])
    ig = _sigmoid(gates[:, LRU_WIDTH:])
    z = -lam_ref[...]
    softplus = jnp.maximum(z, 0.0) + jnp.log1p(jnp.exp(-jnp.abs(z)))
    log_a = (-LRU_C) * r * softplus
    a = jnp.exp(log_a)
    b = jnp.sqrt(jnp.tanh(-log_a) * (1.0 + a * a)) * (ig * xc)

    sub = lax.broadcasted_iota(jnp.int32, (tm, LRU_WIDTH), 0) & (SUBLANES - 1)
    for sh in (1, 2, 4):
        inside = sub >= sh
        a_s = jnp.where(inside, pltpu.roll(a, sh, 0), 1.0)
        b_s = jnp.where(inside, pltpu.roll(b, sh, 0), 0.0)
        b = a * b_s + b
        a = a * a_s
    hp = hcar[0:1, :]
    for j in range(tm // SUBLANES):
        rows = slice(j * SUBLANES, (j + 1) * SUBLANES)
        hj = a[rows, :] * hp + b[rows, :]
        hbuf[rows, :] = hj
        hp = hj[SUBLANES - 1:SUBLANES, :]
    hcar[...] = jnp.broadcast_to(hp, hcar.shape)
    lru_ref[...] = (hbuf[...] * _gelu(y)).astype(BF16)


def _proj_call(h, l, p):
    B, S, _ = h.shape
    tm = TM_PROJ
    grid = (B, S // tm)
    tile = lambda w: pl.BlockSpec((None, tm, w), lambda b, s: (b, s, 0))
    slab = pl.BlockSpec((None, ATTN_PAIRS, tm, LANES), lambda b, s: (b, 0, s, 0))
    lay = lambda *shape: pl.BlockSpec((None,) + shape, lambda b, s: (l,) + (0,) * len(shape))
    slab_shape = jax.ShapeDtypeStruct((B, ATTN_PAIRS, S, LANES), F32)
    return pl.pallas_call(
        _proj_kernel,
        grid=grid,
        in_specs=[tile(D_MODEL), lay(1, D_MODEL), lay(D_MODEL, IN_WIDTH),
                  lay(1, ATTN_WIDTH), lay(1, ATTN_WIDTH), lay(POOL_WIDTH, POOL_WIDTH), lay(1, POOL_WIDTH),
                  lay(LRU_CONV, LRU_WIDTH), lay(1, LRU_WIDTH), lay(LRU_WIDTH, 2 * LRU_WIDTH),
                  lay(1, 2 * LRU_WIDTH), lay(1, LRU_WIDTH)],
        out_specs=[slab, slab, slab, tile(POOL_WIDTH), tile(LRU_WIDTH)],
        out_shape=[slab_shape, slab_shape, slab_shape,
                   jax.ShapeDtypeStruct((B, S, POOL_WIDTH), BF16), jax.ShapeDtypeStruct((B, S, LRU_WIDTH), BF16)],
        scratch_shapes=[pltpu.VMEM((POOL_HALO + tm, POOL_WIDTH), F32), pltpu.VMEM((SUBLANES + tm, LRU_WIDTH), F32),
                        pltpu.VMEM((SUBLANES, LRU_WIDTH), F32), pltpu.VMEM((tm, LRU_WIDTH), F32)],
        compiler_params=pltpu.CompilerParams(dimension_semantics=("arbitrary", "arbitrary"),
                                             vmem_limit_bytes=VMEM_LIMIT_BYTES),
        name="proj",
    )(h, p["norm_mix"], p["w_in"], p["q_gain"], p["k_gain"], p["pool_w"], p["pool_scale"],
      p["lru_conv_w"], p["lru_conv_b"], p["lru_wg"], p["lru_bg"], p["lru_lambda"])


def _attn_kernel(q_ref, k_ref, v_ref, bias_ref, o_ref, qs, ks, vs, oa, ma, la, ob, mb, lb):
    S = o_ref.shape[0]
    nblk = S // ATTN_BLOCK
    Q = ATTN_BLOCK
    low = lax.broadcasted_iota(jnp.int32, (Q, LANES), 1) < HEAD_DIM

    ks[0:Q, :] = jnp.zeros((Q, ATTN_WIDTH), BF16)
    vs[0:Q, :] = jnp.zeros((Q, ATTN_WIDTH), BF16)

    def regroup(src, d_from, step, store):
        Lf, Ln = S // d_from, S // (d_from * step)
        for rf in range(d_from):
            for rs in range(step):
                r_new = rs * d_from + rf
                for c in range(ATTN_PAIRS):
                    rows = pl.ds(rf * Lf + rs, Ln, stride=step) if step > 1 else slice(rf * Lf, (rf + 1) * Lf)
                    store(c, r_new * Ln, Ln, src[c, rows, :])

    def load_qkv(srcs, d_from, step):
        for src, dst, off in zip(srcs, (qs, ks, vs), (0, Q, Q)):
            def store(c, row, n, val, dst=dst, off=off):
                dst[off + row:off + row + n, c * LANES:(c + 1) * LANES] = val.astype(BF16)
            regroup(src, d_from, step, store)

    def reorder_f32(src, dst, d_from, step):
        def store(c, row, n, val):
            dst[c, row:row + n, :] = val
        regroup(src, d_from, step, store)

    def scatter(d_from, d_to, src, dst):
        step = d_from // d_to
        Lf, Lt = S // d_from, S // d_to
        for r in range(d_from):
            start = (r % d_to) * Lt + r // d_to
            for c in range(ATTN_PAIRS):
                dst[c, pl.ds(start, Lf, stride=step), :] = src[c, r * Lf:(r + 1) * Lf, :]

    def process(pidx, d, src, dst):
        nbl = (S // d) // Q

        def body(g, carry):
            r0 = pl.multiple_of(g * Q, Q)
            if nbl > 1:
                first = jnp.where((g & (nbl - 1)) == 0, 1, 0)
            for c in range(ATTN_PAIRS):
                cols = slice(c * LANES, (c + 1) * LANES)
                qb = qs[pl.ds(r0, Q), cols]
                zero = jnp.zeros_like(qb)
                qq = jnp.concatenate([jnp.where(low, qb, zero), jnp.where(low, zero, qb)], axis=0)
                if nbl > 1:
                    kk = ks[pl.ds(r0, 2 * Q), cols]
                    vv = vs[pl.ds(r0, 2 * Q), cols]
                    sc = lax.dot_general(qq, kk, _NT, preferred_element_type=F32) + bias_ref[pidx, first, c]
                else:
                    kk = ks[pl.ds(r0 + Q, Q), cols]
                    vv = vs[pl.ds(r0 + Q, Q), cols]
                    sc = lax.dot_general(qq, kk, _NT, preferred_element_type=F32) + bias_ref[pidx, 1, c, :, Q:2 * Q]
                m = jnp.max(sc, axis=-1, keepdims=True)
                pr = jnp.exp2(sc - m)
                den = jnp.sum(pr, axis=-1, keepdims=True)
                pv = _dot(pr.astype(BF16), vv)
                o_new = jnp.where(low, pv[0:Q, :], pv[Q:2 * Q, :])
                m_new = jnp.where(low, m[0:Q, :], m[Q:2 * Q, :])
                l_new = jnp.where(low, den[0:Q, :], den[Q:2 * Q, :])
                if src is not None:
                    m_old = src[1][c, pl.ds(r0, Q), :]
                    mx = jnp.maximum(m_new, m_old)
                    wn = jnp.exp2(m_new - mx)
                    wo = jnp.exp2(m_old - mx)
                    o_new = wn * o_new + wo * src[0][c, pl.ds(r0, Q), :]
                    l_new = wn * l_new + wo * src[2][c, pl.ds(r0, Q), :]
                    m_new = mx
                if dst is None:
                    o_ref[pl.ds(r0, Q), cols] = (o_new / l_new).astype(BF16)
                else:
                    dst[0][c, pl.ds(r0, Q), :] = o_new
                    dst[1][c, pl.ds(r0, Q), :] = m_new
                    dst[2][c, pl.ds(r0, Q), :] = l_new
            return carry

        lax.fori_loop(0, nblk, body, 0, unroll=ATTN_UNROLL)

    d0, d1, d2 = DILATIONS
    acc_a, acc_b = (oa, ma, la), (ob, mb, lb)
    for src, dst in zip((q_ref, k_ref, v_ref), acc_b):
        reorder_f32(src, dst, d2, d1 // d2)
    load_qkv(acc_b, d1, d0 // d1)
    process(0, d0, None, acc_a)
    load_qkv(acc_b, d1, 1)
    for sa, sb in zip(acc_a, acc_b):
        scatter(d0, d1, sa, sb)
    process(1, d1, acc_b, acc_a)
    for sa, sb in zip(acc_a, acc_b):
        scatter(d1, d2, sa, sb)
    load_qkv((q_ref, k_ref, v_ref), d2, 1)
    process(2, d2, acc_b, None)


def _attn_bias():
    Q = ATTN_BLOCK
    dist = (Q + np.arange(Q)[:, None] - np.arange(2 * Q)[None, :])
    valid = (dist >= 0) & (dist <= Q)
    has_prev = np.arange(2 * Q)[None, :] >= Q
    slopes = np.asarray([2.0 ** (-8.0 * (h + 1) / ATTN_HEADS) for h in range(ATTN_HEADS)], dtype=np.float32)
    out = np.empty((len(DILATIONS), 2, ATTN_PAIRS, 2 * Q, 2 * Q), np.float32)
    for pi, d in enumerate(DILATIONS):
        for h in range(ATTN_HEADS):
            step = slopes[h] * np.float32(d)
            pen = (step * dist.astype(np.float32)).astype(np.float64) * LOG2E
            rows = slice((h % 2) * Q, (h % 2 + 1) * Q)
            out[pi, 0, h // 2, rows, :] = np.where(valid, -pen, NEG)
            out[pi, 1, h // 2, rows, :] = np.where(valid & has_prev, -pen, NEG)
    return jnp.asarray(out)


def _attn_call(q, k, v, bias):
    B, _, S, _ = q.shape
    slab = pl.BlockSpec((None, ATTN_PAIRS, S, LANES), lambda b: (b, 0, 0, 0))
    acc = pltpu.VMEM((ATTN_PAIRS, S, LANES), F32)
    return pl.pallas_call(
        _attn_kernel,
        grid=(B,),
        in_specs=[slab, slab, slab,
                  pl.BlockSpec(bias.shape, lambda b: (0, 0, 0, 0, 0), pipeline_mode=pl.Buffered(1))],
        out_specs=pl.BlockSpec((None, S, ATTN_WIDTH), lambda b: (b, 0, 0)),
        out_shape=jax.ShapeDtypeStruct((B, S, ATTN_WIDTH), BF16),
        scratch_shapes=[pltpu.VMEM((S, ATTN_WIDTH), BF16), pltpu.VMEM((S + ATTN_BLOCK, ATTN_WIDTH), BF16),
                        pltpu.VMEM((S + ATTN_BLOCK, ATTN_WIDTH), BF16), acc, acc, acc, acc, acc, acc],
        compiler_params=pltpu.CompilerParams(dimension_semantics=("arbitrary",),
                                             vmem_limit_bytes=VMEM_LIMIT_BYTES),
        name="attn",
    )(q, k, v, bias)


def _memkv_kernel(mem_ref, g_ref, w_ref, kg_ref, k_ref, v_ref):
    mn = _rms(mem_ref[...], g_ref[...]).astype(BF16)
    kv = _dot(mn, w_ref[...])
    for hh in range(MEM_HEADS):
        cols = slice(hh * MEM_HEAD_DIM, (hh + 1) * MEM_HEAD_DIM)
        k_ref[:, cols] = _rms(kv[:, cols], kg_ref[...]).astype(BF16)
    v_ref[...] = kv[:, D_MODEL:].astype(BF16)


def _memkv_call(mem, p):
    B, M, _ = mem.shape
    L = p["w_kv_mem"].shape[0]
    lay = lambda *shape: pl.BlockSpec((None,) + shape, lambda l, b: (l,) + (0,) * len(shape))
    out = pl.BlockSpec((None, None, M, D_MODEL), lambda l, b: (l, b, 0, 0))
    shape = jax.ShapeDtypeStruct((L, B, M, D_MODEL), BF16)
    return pl.pallas_call(
        _memkv_kernel,
        grid=(L, B),
        in_specs=[pl.BlockSpec((None, M, D_MODEL), lambda l, b: (b, 0, 0)), lay(1, D_MODEL),
                  lay(D_MODEL, 2 * D_MODEL), lay(1, MEM_HEAD_DIM)],
        out_specs=[out, out],
        out_shape=[shape, shape],
        compiler_params=pltpu.CompilerParams(dimension_semantics=("arbitrary", "arbitrary"),
                                             vmem_limit_bytes=VMEM_LIMIT_BYTES),
        name="memkv",
    )(mem, p["norm_memkv"], p["w_kv_mem"], p["mk_gain"])


def _mixmem_kernel(pool_ref, attn_ref, lru_ref, h_ref, wout_ref, g_ref, wq_ref, qg_ref, k_ref, v_ref, wo_ref,
                   o_ref, mix, obuf):
    mix[:, 0:POOL_WIDTH] = pool_ref[...]
    mix[:, POOL_WIDTH:POOL_WIDTH + ATTN_WIDTH] = attn_ref[...]
    mix[:, POOL_WIDTH + ATTN_WIDTH:] = lru_ref[...]
    h = h_ref[...] + _dot(mix[...], wout_ref[...])
    q = _dot(_rms(h, g_ref[...]).astype(BF16), wq_ref[...])
    for hh in range(MEM_HEADS):
        cols = slice(hh * MEM_HEAD_DIM, (hh + 1) * MEM_HEAD_DIM)
        qh = _rms(q[:, cols], qg_ref[...]).astype(BF16)
        sc = lax.dot_general(qh, k_ref[:, cols], _NT, preferred_element_type=F32)
        m = jnp.max(sc, axis=-1, keepdims=True)
        pr = jnp.exp(sc - m)
        pr = pr / jnp.sum(pr, axis=-1, keepdims=True)
        obuf[:, cols] = _dot(pr.astype(BF16), v_ref[:, cols]).astype(BF16)
    o_ref[...] = h + _dot(obuf[...], wo_ref[...])


def _mixmem_call(pool, attn, lru, h, kmem, vmem, l, p):
    B, S, _ = h.shape
    M = kmem.shape[2]
    tm = TM_MEM
    tile = lambda w: pl.BlockSpec((None, tm, w), lambda b, s: (b, s, 0))
    lay = lambda *shape: pl.BlockSpec((None,) + shape, lambda b, s: (l,) + (0,) * len(shape),
                                      pipeline_mode=pl.Buffered(1))
    kv = pl.BlockSpec((None, None, M, D_MODEL), lambda b, s: (l, b, 0, 0))
    return pl.pallas_call(
        _mixmem_kernel,
        grid=(B, S // tm),
        in_specs=[tile(POOL_WIDTH), tile(ATTN_WIDTH), tile(LRU_WIDTH), tile(D_MODEL), lay(D_MODEL, D_MODEL),
                  lay(1, D_MODEL), lay(D_MODEL, D_MODEL), lay(1, MEM_HEAD_DIM), kv, kv, lay(D_MODEL, D_MODEL)],
        out_specs=tile(D_MODEL),
        out_shape=jax.ShapeDtypeStruct(h.shape, F32),
        scratch_shapes=[pltpu.VMEM((tm, D_MODEL), BF16), pltpu.VMEM((tm, D_MODEL), BF16)],
        compiler_params=pltpu.CompilerParams(dimension_semantics=("arbitrary", "arbitrary"),
                                             vmem_limit_bytes=VMEM_LIMIT_BYTES),
        name="mixmem",
    )(pool, attn, lru, h, p["w_out"], p["norm_mem"], p["w_q_mem"], p["mq_gain"], kmem, vmem, p["w_o_mem"])


def _ffn_kernel(h_ref, g_ref, wup_ref, cw_ref, cb_ref, wdn_ref, o_ref, gcar, act):
    s = pl.program_id(1)
    tm = h_ref.shape[0]

    @pl.when(s == 0)
    def _():
        gcar[...] = jnp.zeros(gcar.shape, F32)

    h = h_ref[...]
    hn = _rms(h, g_ref[...]).astype(BF16)
    for c in range(D_FF // FFN_CHUNK):
        cols = slice(c * FFN_CHUNK, (c + 1) * FFN_CHUNK)
        g = _dot(hn, wup_ref[:, cols])
        u = _dot(hn, wup_ref[:, D_FF + c * FFN_CHUNK:D_FF + (c + 1) * FFN_CHUNK])
        ge = jnp.concatenate([gcar[:, cols], g], axis=0)
        conv = cb_ref[:, cols] + cw_ref[0:1, cols] * ge
        for kk in range(1, FFN_CONV):
            conv = conv + cw_ref[kk:kk + 1, cols] * pltpu.roll(ge, kk, 0)
        gcar[:, cols] = g[tm - SUBLANES:tm, :]
        act[:, cols] = (_gelu(conv[SUBLANES:, :]) * u).astype(BF16)
    o_ref[...] = h + _dot(act[...], wdn_ref[...])


def _ffn_call(h, l, p):
    B, S, _ = h.shape
    tm = TM_FFN
    tile = pl.BlockSpec((None, tm, D_MODEL), lambda b, s: (b, s, 0))
    lay = lambda *shape: pl.BlockSpec((None,) + shape, lambda b, s: (l,) + (0,) * len(shape),
                                      pipeline_mode=pl.Buffered(1))
    return pl.pallas_call(
        _ffn_kernel,
        grid=(B, S // tm),
        in_specs=[tile, lay(1, D_MODEL), lay(D_MODEL, 2 * D_FF), lay(FFN_CONV, D_FF), lay(1, D_FF),
                  lay(D_FF, D_MODEL)],
        out_specs=tile,
        out_shape=jax.ShapeDtypeStruct(h.shape, F32),
        scratch_shapes=[pltpu.VMEM((SUBLANES, D_FF), F32), pltpu.VMEM((tm, D_FF), BF16)],
        compiler_params=pltpu.CompilerParams(dimension_semantics=("arbitrary", "arbitrary"),
                                             vmem_limit_bytes=VMEM_LIMIT_BYTES),
        name="ffn",
    )(h, p["norm_ffn"], p["w_up"], p["ffn_conv_w"], p["ffn_conv_b"], p["w_down"])


def _block_diag(w):
    L, G, n, _ = w.shape
    eye = jnp.eye(G, dtype=w.dtype)
    return jnp.einsum("lgij,gh->lgihj", w, eye).reshape(L, G * n, G * n)


def kernel(x, mem, norm_mix, w_in, pool_w, pool_scale, q_gain, k_gain, lru_conv_w, lru_conv_b, lru_wa, lru_ba,
           lru_wx, lru_bx, lru_lambda, w_out, norm_mem, norm_memkv, w_q_mem, w_kv_mem, mq_gain, mk_gain, w_o_mem,
           norm_ffn, w_up, ffn_conv_w, ffn_conv_b, w_down):
    depth = w_in.shape[0]
    row = lambda a: a.astype(F32)[:, None, :]
    p = {
        "norm_mix": row(norm_mix),
        "w_in": w_in.astype(BF16),
        "q_gain": row(jnp.tile(q_gain, (1, ATTN_HEADS)) * (HEAD_DIM ** -0.5 * LOG2E)),
        "k_gain": row(jnp.tile(k_gain, (1, ATTN_HEADS))),
        "pool_w": _block_diag(pool_w).astype(BF16),
        "pool_scale": row(pool_scale),
        "lru_conv_w": lru_conv_w.astype(F32),
        "lru_conv_b": row(lru_conv_b),
        "lru_wg": jnp.concatenate([_block_diag(lru_wa), _block_diag(lru_wx)], axis=-1).astype(BF16),
        "lru_bg": row(jnp.concatenate([lru_ba, lru_bx], axis=-1)),
        "lru_lambda": row(lru_lambda),
        "w_out": w_out.astype(BF16),
        "norm_mem": row(norm_mem),
        "norm_memkv": row(norm_memkv),
        "w_q_mem": w_q_mem.astype(BF16),
        "w_kv_mem": w_kv_mem.astype(BF16),
        "mq_gain": row(mq_gain * (MEM_HEAD_DIM ** -0.5)),
        "mk_gain": row(mk_gain),
        "w_o_mem": w_o_mem.astype(BF16),
        "norm_ffn": row(norm_ffn),
        "w_up": w_up.astype(BF16),
        "ffn_conv_w": ffn_conv_w.astype(F32),
        "ffn_conv_b": row(ffn_conv_b),
        "w_down": w_down.astype(BF16),
    }
    bias = _attn_bias()
    kmem, vmem = _memkv_call(mem, p)
    h = x
    for l in range(depth):
        q, k, v, pool, lru = _proj_call(h, l, p)
        attn = _attn_call(q, k, v, bias)
        h = _mixmem_call(pool, attn, lru, h, kmem, vmem, l, p)
        h = _ffn_call(h, l, p)
    return h
```

```python
import numpy as np
import jax
import jax.numpy as jnp
from jax import lax
from jax.experimental import pallas as pl
from jax.experimental.pallas import tpu as pltpu

F32 = jnp.float32
BF16 = jnp.bfloat16

LANES = 128
SUBLANES = 8
VMEM_LIMIT_BYTES = 56 * 1024 * 1024

D_MODEL = 1024
POOL_WIDTH = 256
POOL_GROUPS = 4
POOL_WINDOWS = (2, 4, 8, 16)
POOL_HALO = 16
HEAD_DIM = 64
ATTN_WIDTH = 384
ATTN_HEADS = 6
ATTN_PAIRS = ATTN_WIDTH // LANES
DILATIONS = (16, 4, 1)
ATTN_BLOCK = 128
ATTN_UNROLL = 8
LRU_WIDTH = 384
LRU_BLOCKS = 6
LRU_CONV = 4
LRU_C = 8.0
IN_WIDTH = POOL_WIDTH + 3 * ATTN_WIDTH + 2 * LRU_WIDTH
MEM_HEADS = 4
MEM_HEAD_DIM = D_MODEL // MEM_HEADS
D_FF = 2816
FFN_CONV = 3
FFN_CHUNK = 256
EPS = 1e-6
F32_TINY = 1.1754944e-38
NEG = -1e30
LOG2E = 1.4426950408889634

TM_PROJ = 512
TM_MEM = 1024
TM_FFN = 1024

_NT = (((1,), (1,)), ((), ()))


def _rms(x, g):
    ms = jnp.mean(x * x, axis=-1, keepdims=True)
    return x * lax.rsqrt(ms + EPS) * g


def _gelu(x):
    c = 0.7978845608028654
    hx = 0.5 * x
    return hx + hx * jnp.tanh(x * (c + (c * 0.044715) * (x * x)))


def _dot(a, b):
    return jnp.dot(a, b, preferred_element_type=F32)


def _proj_kernel(h_ref, gmix_ref, win_ref, qg_ref, kg_ref, poolw_ref, pscale_ref,
                 cw_ref, cb_ref, wg_ref, bg_ref, lam_ref,
                 q_ref, k_ref, v_ref, pool_ref, lru_ref,
                 uext, xext, hcar, hbuf):
    s = pl.program_id(1)
    tm = h_ref.shape[0]

    @pl.when(s == 0)
    def _():
        uext[0:POOL_HALO, :] = jnp.zeros((POOL_HALO, POOL_WIDTH), F32)
        xext[0:SUBLANES, :] = jnp.zeros((SUBLANES, LRU_WIDTH), F32)
        hcar[...] = jnp.zeros(hcar.shape, F32)

    hn = _rms(h_ref[...], gmix_ref[...]).astype(BF16)
    proj = _dot(hn, win_ref[...])
    o = POOL_WIDTH
    u = proj[:, 0:o]
    q = proj[:, o:o + ATTN_WIDTH]
    k = proj[:, o + ATTN_WIDTH:o + 2 * ATTN_WIDTH]
    v = proj[:, o + 2 * ATTN_WIDTH:o + 3 * ATTN_WIDTH]
    x = proj[:, o + 3 * ATTN_WIDTH:o + 3 * ATTN_WIDTH + LRU_WIDTH]
    y = proj[:, o + 3 * ATTN_WIDTH + LRU_WIDTH:]

    low_head = lax.broadcasted_iota(jnp.int32, (tm, LANES), 1) < HEAD_DIM

    def headnorm(t, gain):
        t2 = t * t
        parts = []
        for c in range(ATTN_PAIRS):
            blk = t2[:, c * LANES:(c + 1) * LANES]
            even = jnp.sum(jnp.where(low_head, blk, 0.0), axis=-1, keepdims=True)
            odd = jnp.sum(jnp.where(low_head, 0.0, blk), axis=-1, keepdims=True)
            parts.append(jnp.where(low_head, even, odd))
        ssq = jnp.concatenate(parts, axis=1)
        return t * lax.rsqrt(ssq + HEAD_DIM * EPS) * gain

    qn = headnorm(q, qg_ref[...])
    kn = headnorm(k, kg_ref[...])
    for c in range(ATTN_PAIRS):
        sl = slice(c * LANES, (c + 1) * LANES)
        q_ref[c] = qn[:, sl]
        k_ref[c] = kn[:, sl]
        v_ref[c] = v[:, sl]

    uext[POOL_HALO:POOL_HALO + tm, :] = u
    e = uext[...]
    s2 = e + pltpu.roll(e, 1, 0)
    s4 = s2 + pltpu.roll(s2, 2, 0)
    s8 = s4 + pltpu.roll(s4, 4, 0)
    s16 = s8 + pltpu.roll(s8, 8, 0)
    uext[0:POOL_HALO, :] = u[tm - POOL_HALO:tm, :]
    tp1 = (s * tm + 1 + lax.broadcasted_iota(jnp.int32, (tm, LANES), 0)).astype(F32)
    low = lax.broadcasted_iota(jnp.int32, (tm, LANES), 1) < POOL_WIDTH // POOL_GROUPS

    def wmean(sw, col, w):
        return sw[POOL_HALO:, col * LANES:(col + 1) * LANES] / jnp.minimum(tp1, float(w))

    pooled = jnp.concatenate(
        [jnp.where(low, wmean(s2, 0, POOL_WINDOWS[0]), wmean(s4, 0, POOL_WINDOWS[1])) - u[:, 0:LANES],
         jnp.where(low, wmean(s8, 1, POOL_WINDOWS[2]), wmean(s16, 1, POOL_WINDOWS[3])) - u[:, LANES:2 * LANES]],
        axis=1)
    pool_ref[...] = (_dot(pooled.astype(BF16), poolw_ref[...]) * pscale_ref[...]).astype(BF16)

    xext[SUBLANES:SUBLANES + tm, :] = x
    ex = xext[...]
    cw = cw_ref[...]
    conv = cb_ref[...] + cw[0:1, :] * ex
    for kk in range(1, LRU_CONV):
        conv = conv + cw[kk:kk + 1, :] * pltpu.roll(ex, kk, 0)
    xc = conv[SUBLANES:, :]
    xext[0:SUBLANES, :] = x[tm - SUBLANES:tm, :]

    xcb = xc.astype(BF16)
    th_a, th_x = [], []
    for c in range(LRU_WIDTH // LANES):
        th = jnp.tanh(_dot(xcb[:, c * LANES:(c + 1) * LANES], wg_ref[c]) + bg_ref[c])
        th_a.append(th[:, 0:LANES])
        th_x.append(th[:, LANES:])
    ig = 0.5 * jnp.concatenate(th_x, axis=1) + 0.5
    z = -lam_ref[...]
    softplus = jnp.maximum(z, 0.0) + jnp.log1p(jnp.exp(-jnp.abs(z)))
    half = (-0.5 * LRU_C) * softplus
    log_a = half * jnp.concatenate(th_a, axis=1) + half
    a = jnp.exp(log_a)
    t = jnp.tanh(-log_a) * (1.0 + a * a)
    b = (t * lax.rsqrt(jnp.maximum(t, F32_TINY))) * (ig * xc)

    ngroups = tm // SUBLANES
    a = a.reshape(ngroups, SUBLANES, LRU_WIDTH)
    b = b.reshape(ngroups, SUBLANES, LRU_WIDTH)
    sub = lax.broadcasted_iota(jnp.int32, (ngroups, SUBLANES, LRU_WIDTH), 1)
    for sh in (1, 2, 4):
        inside = sub >= sh
        a_s = jnp.where(inside, pltpu.roll(a, sh, 1), 1.0)
        b_s = jnp.where(inside, pltpu.roll(b, sh, 1), 0.0)
        b = a * b_s + b
        a = a * a_s
    hp = hcar[0:1, :]
    for j in range(ngroups):
        rows = slice(j * SUBLANES, (j + 1) * SUBLANES)
        hj = a[j] * hp + b[j]
        hbuf[rows, :] = hj
        hp = hj[SUBLANES - 1:SUBLANES, :]
    hcar[...] = jnp.broadcast_to(hp, hcar.shape)
    lru_ref[...] = (hbuf[...] * _gelu(y)).astype(BF16)


def _proj_call(h, l, p):
    B, S, _ = h.shape
    tm = TM_PROJ
    grid = (B, S // tm)
    tile = lambda w: pl.BlockSpec((None, tm, w), lambda b, s: (b, s, 0))
    slab = pl.BlockSpec((None, ATTN_PAIRS, tm, LANES), lambda b, s: (b, 0, s, 0))
    lay = lambda *shape: pl.BlockSpec((None,) + shape, lambda b, s: (l,) + (0,) * len(shape))
    slab_shape = jax.ShapeDtypeStruct((B, ATTN_PAIRS, S, LANES), F32)
    return pl.pallas_call(
        _proj_kernel,
        grid=grid,
        in_specs=[tile(D_MODEL), lay(1, D_MODEL), lay(D_MODEL, IN_WIDTH),
                  lay(1, ATTN_WIDTH), lay(1, ATTN_WIDTH), lay(POOL_WIDTH, POOL_WIDTH), lay(1, POOL_WIDTH),
                  lay(LRU_CONV, LRU_WIDTH), lay(1, LRU_WIDTH), lay(LRU_WIDTH // LANES, LANES, 2 * LANES),
                  lay(LRU_WIDTH // LANES, 1, 2 * LANES), lay(1, LRU_WIDTH)],
        out_specs=[slab, slab, slab, tile(POOL_WIDTH), tile(LRU_WIDTH)],
        out_shape=[slab_shape, slab_shape, slab_shape,
                   jax.ShapeDtypeStruct((B, S, POOL_WIDTH), BF16), jax.ShapeDtypeStruct((B, S, LRU_WIDTH), BF16)],
        scratch_shapes=[pltpu.VMEM((POOL_HALO + tm, POOL_WIDTH), F32), pltpu.VMEM((SUBLANES + tm, LRU_WIDTH), F32),
                        pltpu.VMEM((SUBLANES, LRU_WIDTH), F32), pltpu.VMEM((tm, LRU_WIDTH), F32)],
        compiler_params=pltpu.CompilerParams(dimension_semantics=("arbitrary", "arbitrary"),
                                             vmem_limit_bytes=VMEM_LIMIT_BYTES),
        name="proj",
    )(h, p["norm_mix"], p["w_in"], p["q_gain"], p["k_gain"], p["pool_w"], p["pool_scale"],
      p["lru_conv_w"], p["lru_conv_b"], p["lru_wg"], p["lru_bg"], p["lru_lambda"])


def _attn_kernel(q_ref, k_ref, v_ref, bias_ref, o_ref, qs, ks, vs, oa, ma, la, ob, mb, lb):
    S = o_ref.shape[0]
    nblk = S // ATTN_BLOCK
    Q = ATTN_BLOCK
    low = lax.broadcasted_iota(jnp.int32, (Q, LANES), 1) < HEAD_DIM

    ks[0:Q, :] = jnp.zeros((Q, ATTN_WIDTH), BF16)
    vs[0:Q, :] = jnp.zeros((Q, ATTN_WIDTH), BF16)

    def regroup(src, d_from, step, store):
        Lf, Ln = S // d_from, S // (d_from * step)
        for rf in range(d_from):
            for rs in range(step):
                r_new = rs * d_from + rf
                for c in range(ATTN_PAIRS):
                    rows = pl.ds(rf * Lf + rs, Ln, stride=step) if step > 1 else slice(rf * Lf, (rf + 1) * Lf)
                    store(c, r_new * Ln, Ln, src[c, rows, :])

    def load_qkv(srcs, d_from, step):
        for src, dst, off in zip(srcs, (qs, ks, vs), (0, Q, Q)):
            def store(c, row, n, val, dst=dst, off=off):
                dst[off + row:off + row + n, c * LANES:(c + 1) * LANES] = val.astype(BF16)
            regroup(src, d_from, step, store)

    def reorder_f32(src, dst, d_from, step):
        def store(c, row, n, val):
            dst[c, row:row + n, :] = val
        regroup(src, d_from, step, store)

    def scatter(d_from, d_to, src, dst):
        step = d_from // d_to
        Lf, Lt = S // d_from, S // d_to
        for r in range(d_from):
            start = (r % d_to) * Lt + r // d_to
            for c in range(ATTN_PAIRS):
                dst[c, pl.ds(start, Lf, stride=step), :] = src[c, r * Lf:(r + 1) * Lf, :]

    def process(pidx, d, src, dst):
        nbl = (S // d) // Q

        def body(g, carry):
            r0 = pl.multiple_of(g * Q, Q)
            if nbl > 1:
                first = jnp.where((g & (nbl - 1)) == 0, 1, 0)
            for c in range(ATTN_PAIRS):
                cols = slice(c * LANES, (c + 1) * LANES)
                qb = qs[pl.ds(r0, Q), cols]
                zero = jnp.zeros_like(qb)
                qq = jnp.concatenate([jnp.where(low, qb, zero), jnp.where(low, zero, qb)], axis=0)
                if nbl > 1:
                    kk = ks[pl.ds(r0, 2 * Q), cols]
                    vv = vs[pl.ds(r0, 2 * Q), cols]
                    sc = lax.dot_general(qq, kk, _NT, preferred_element_type=F32) + bias_ref[pidx, first, c]
                else:
                    kk = ks[pl.ds(r0 + Q, Q), cols]
                    vv = vs[pl.ds(r0 + Q, Q), cols]
                    sc = lax.dot_general(qq, kk, _NT, preferred_element_type=F32) + bias_ref[pidx, 1, c, :, Q:2 * Q]
                m = jnp.max(sc, axis=-1, keepdims=True)
                pr = jnp.exp2(sc - m)
                den = jnp.sum(pr, axis=-1, keepdims=True)
                pv = _dot(pr.astype(BF16), vv)
                o_new = jnp.where(low, pv[0:Q, :], pv[Q:2 * Q, :])
                m_new = jnp.where(low, m[0:Q, :], m[Q:2 * Q, :])
                l_new = jnp.where(low, den[0:Q, :], den[Q:2 * Q, :])
                if src is not None:
                    m_old = src[1][c, pl.ds(r0, Q), :]
                    mx = jnp.maximum(m_new, m_old)
                    wn = jnp.exp2(m_new - mx)
                    wo = jnp.exp2(m_old - mx)
                    o_new = wn * o_new + wo * src[0][c, pl.ds(r0, Q), :]
                    l_new = wn * l_new + wo * src[2][c, pl.ds(r0, Q), :]
                    m_new = mx
                if dst is None:
                    o_ref[pl.ds(r0, Q), cols] = (o_new / l_new).astype(BF16)
                else:
                    dst[0][c, pl.ds(r0, Q), :] = o_new
                    dst[1][c, pl.ds(r0, Q), :] = m_new
                    dst[2][c, pl.ds(r0, Q), :] = l_new
            return carry

        lax.fori_loop(0, nblk, body, 0, unroll=ATTN_UNROLL)

    d0, d1, d2 = DILATIONS
    acc_a, acc_b = (oa, ma, la), (ob, mb, lb)
    for src, dst in zip((q_ref, k_ref, v_ref), acc_b):
        reorder_f32(src, dst, d2, d1 // d2)
    load_qkv(acc_b, d1, d0 // d1)
    process(0, d0, None, acc_a)
    load_qkv(acc_b, d1, 1)
    for sa, sb in zip(acc_a, acc_b):
        scatter(d0, d1, sa, sb)
    process(1, d1, acc_b, acc_a)
    for sa, sb in zip(acc_a, acc_b):
        scatter(d1, d2, sa, sb)
    load_qkv((q_ref, k_ref, v_ref), d2, 1)
    process(2, d2, acc_b, None)


def _attn_bias():
    Q = ATTN_BLOCK
    dist = (Q + np.arange(Q)[:, None] - np.arange(2 * Q)[None, :])
    valid = (dist >= 0) & (dist <= Q)
    has_prev = np.arange(2 * Q)[None, :] >= Q
    slopes = np.asarray([2.0 ** (-8.0 * (h + 1) / ATTN_HEADS) for h in range(ATTN_HEADS)], dtype=np.float32)
    out = np.empty((len(DILATIONS), 2, ATTN_PAIRS, 2 * Q, 2 * Q), np.float32)
    for pi, d in enumerate(DILATIONS):
        for h in range(ATTN_HEADS):
            step = slopes[h] * np.float32(d)
            pen = (step * dist.astype(np.float32)).astype(np.float64) * LOG2E
            rows = slice((h % 2) * Q, (h % 2 + 1) * Q)
            out[pi, 0, h // 2, rows, :] = np.where(valid, -pen, NEG)
            out[pi, 1, h // 2, rows, :] = np.where(valid & has_prev, -pen, NEG)
    return jnp.asarray(out)


def _attn_call(q, k, v, bias):
    B, _, S, _ = q.shape
    slab = pl.BlockSpec((None, ATTN_PAIRS, S, LANES), lambda b: (b, 0, 0, 0))
    acc = pltpu.VMEM((ATTN_PAIRS, S, LANES), F32)
    return pl.pallas_call(
        _attn_kernel,
        grid=(B,),
        in_specs=[slab, slab, slab,
                  pl.BlockSpec(bias.shape, lambda b: (0, 0, 0, 0, 0), pipeline_mode=pl.Buffered(1))],
        out_specs=pl.BlockSpec((None, S, ATTN_WIDTH), lambda b: (b, 0, 0)),
        out_shape=jax.ShapeDtypeStruct((B, S, ATTN_WIDTH), BF16),
        scratch_shapes=[pltpu.VMEM((S, ATTN_WIDTH), BF16), pltpu.VMEM((S + ATTN_BLOCK, ATTN_WIDTH), BF16),
                        pltpu.VMEM((S + ATTN_BLOCK, ATTN_WIDTH), BF16), acc, acc, acc, acc, acc, acc],
        compiler_params=pltpu.CompilerParams(dimension_semantics=("arbitrary",),
                                             vmem_limit_bytes=VMEM_LIMIT_BYTES),
        name="attn",
    )(q, k, v, bias)


def _memkv_kernel(mem_ref, g_ref, w_ref, kg_ref, k_ref, v_ref):
    mn = _rms(mem_ref[...], g_ref[...]).astype(BF16)
    kv = _dot(mn, w_ref[...])
    for hh in range(MEM_HEADS):
        cols = slice(hh * MEM_HEAD_DIM, (hh + 1) * MEM_HEAD_DIM)
        k_ref[:, cols] = _rms(kv[:, cols], kg_ref[...]).astype(BF16)
    v_ref[...] = kv[:, D_MODEL:].astype(BF16)


def _memkv_call(mem, p):
    B, M, _ = mem.shape
    L = p["w_kv_mem"].shape[0]
    lay = lambda *shape: pl.BlockSpec((None,) + shape, lambda l, b: (l,) + (0,) * len(shape))
    out = pl.BlockSpec((None, None, M, D_MODEL), lambda l, b: (l, b, 0, 0))
    shape = jax.ShapeDtypeStruct((L, B, M, D_MODEL), BF16)
    return pl.pallas_call(
        _memkv_kernel,
        grid=(L, B),
        in_specs=[pl.BlockSpec((None, M, D_MODEL), lambda l, b: (b, 0, 0)), lay(1, D_MODEL),
                  lay(D_MODEL, 2 * D_MODEL), lay(1, MEM_HEAD_DIM)],
        out_specs=[out, out],
        out_shape=[shape, shape],
        compiler_params=pltpu.CompilerParams(dimension_semantics=("arbitrary", "arbitrary"),
                                             vmem_limit_bytes=VMEM_LIMIT_BYTES),
        name="memkv",
    )(mem, p["norm_memkv"], p["w_kv_mem"], p["mk_gain"])


def _mixmem_kernel(pool_ref, attn_ref, lru_ref, h_ref, wout_ref, g_ref, wq_ref, qg_ref, k_ref, v_ref, wo_ref,
                   o_ref, mix, obuf):
    mix[:, 0:POOL_WIDTH] = pool_ref[...]
    mix[:, POOL_WIDTH:POOL_WIDTH + ATTN_WIDTH] = attn_ref[...]
    mix[:, POOL_WIDTH + ATTN_WIDTH:] = lru_ref[...]
    h = h_ref[...] + _dot(mix[...], wout_ref[...])
    q = _dot(_rms(h, g_ref[...]).astype(BF16), wq_ref[...])
    for hh in range(MEM_HEADS):
        cols = slice(hh * MEM_HEAD_DIM, (hh + 1) * MEM_HEAD_DIM)
        qh = _rms(q[:, cols], qg_ref[...]).astype(BF16)
        sc = lax.dot_general(qh, k_ref[:, cols], _NT, preferred_element_type=F32)
        m = jnp.max(sc, axis=-1, keepdims=True)
        pr = jnp.exp(sc - m)
        pr = pr / jnp.sum(pr, axis=-1, keepdims=True)
        obuf[:, cols] = _dot(pr.astype(BF16), v_ref[:, cols]).astype(BF16)
    o_ref[...] = h + _dot(obuf[...], wo_ref[...])


def _mixmem_call(pool, attn, lru, h, kmem, vmem, l, p):
    B, S, _ = h.shape
    M = kmem.shape[2]
    tm = TM_MEM
    tile = lambda w: pl.BlockSpec((None, tm, w), lambda b, s: (b, s, 0))
    lay = lambda *shape: pl.BlockSpec((None,) + shape, lambda b, s: (l,) + (0,) * len(shape),
                                      pipeline_mode=pl.Buffered(1))
    kv = pl.BlockSpec((None, None, M, D_MODEL), lambda b, s: (l, b, 0, 0))
    return pl.pallas_call(
        _mixmem_kernel,
        grid=(B, S // tm),
        in_specs=[tile(POOL_WIDTH), tile(ATTN_WIDTH), tile(LRU_WIDTH), tile(D_MODEL), lay(D_MODEL, D_MODEL),
                  lay(1, D_MODEL), lay(D_MODEL, D_MODEL), lay(1, MEM_HEAD_DIM), kv, kv, lay(D_MODEL, D_MODEL)],
        out_specs=tile(D_MODEL),
        out_shape=jax.ShapeDtypeStruct(h.shape, F32),
        scratch_shapes=[pltpu.VMEM((tm, D_MODEL), BF16), pltpu.VMEM((tm, D_MODEL), BF16)],
        compiler_params=pltpu.CompilerParams(dimension_semantics=("arbitrary", "arbitrary"),
                                             vmem_limit_bytes=VMEM_LIMIT_BYTES),
        name="mixmem",
    )(pool, attn, lru, h, p["w_out"], p["norm_mem"], p["w_q_mem"], p["mq_gain"], kmem, vmem, p["w_o_mem"])


def _ffn_kernel(h_ref, g_ref, wup_ref, cw_ref, cb_ref, wdn_ref, o_ref, gcar, act):
    s = pl.program_id(1)
    tm = h_ref.shape[0]

    @pl.when(s == 0)
    def _():
        gcar[...] = jnp.zeros(gcar.shape, F32)

    h = h_ref[...]
    hn = _rms(h, g_ref[...]).astype(BF16)
    for c in range(D_FF // FFN_CHUNK):
        cols = slice(c * FFN_CHUNK, (c + 1) * FFN_CHUNK)
        g = _dot(hn, wup_ref[:, cols])
        u = _dot(hn, wup_ref[:, D_FF + c * FFN_CHUNK:D_FF + (c + 1) * FFN_CHUNK])
        ge = jnp.concatenate([gcar[:, cols], g], axis=0)
        conv = cb_ref[:, cols] + cw_ref[0:1, cols] * ge
        for kk in range(1, FFN_CONV):
            conv = conv + cw_ref[kk:kk + 1, cols] * pltpu.roll(ge, kk, 0)
        gcar[:, cols] = g[tm - SUBLANES:tm, :]
        act[:, cols] = (_gelu(conv[SUBLANES:, :]) * u).astype(BF16)
    o_ref[...] = h + _dot(act[...], wdn_ref[...])


def _ffn_call(h, l, p):
    B, S, _ = h.shape
    tm = TM_FFN
    tile = pl.BlockSpec((None, tm, D_MODEL), lambda b, s: (b, s, 0))
    lay = lambda *shape: pl.BlockSpec((None,) + shape, lambda b, s: (l,) + (0,) * len(shape),
                                      pipeline_mode=pl.Buffered(1))
    return pl.pallas_call(
        _ffn_kernel,
        grid=(B, S // tm),
        in_specs=[tile, lay(1, D_MODEL), lay(D_MODEL, 2 * D_FF), lay(FFN_CONV, D_FF), lay(1, D_FF),
                  lay(D_FF, D_MODEL)],
        out_specs=tile,
        out_shape=jax.ShapeDtypeStruct(h.shape, F32),
        scratch_shapes=[pltpu.VMEM((SUBLANES, D_FF), F32), pltpu.VMEM((tm, D_FF), BF16)],
        compiler_params=pltpu.CompilerParams(dimension_semantics=("arbitrary", "arbitrary"),
                                             vmem_limit_bytes=VMEM_LIMIT_BYTES),
        name="ffn",
    )(h, p["norm_ffn"], p["w_up"], p["ffn_conv_w"], p["ffn_conv_b"], p["w_down"])


def _block_diag(w):
    L, G, n, _ = w.shape
    eye = jnp.eye(G, dtype=w.dtype)
    return jnp.einsum("lgij,gh->lgihj", w, eye).reshape(L, G * n, G * n)


def _slab_diag(w):
    L, G, n, _ = w.shape
    per = LANES // n
    return _block_diag(w.reshape(L * (G // per), per, n, n)).reshape(L, G // per, LANES, LANES)


def kernel(x, mem, norm_mix, w_in, pool_w, pool_scale, q_gain, k_gain, lru_conv_w, lru_conv_b, lru_wa, lru_ba,
           lru_wx, lru_bx, lru_lambda, w_out, norm_mem, norm_memkv, w_q_mem, w_kv_mem, mq_gain, mk_gain, w_o_mem,
           norm_ffn, w_up, ffn_conv_w, ffn_conv_b, w_down):
    depth = w_in.shape[0]
    row = lambda a: a.astype(F32)[:, None, :]
    p = {
        "norm_mix": row(norm_mix),
        "w_in": w_in.astype(BF16),
        "q_gain": row(jnp.tile(q_gain, (1, ATTN_HEADS)) * LOG2E),
        "k_gain": row(jnp.tile(k_gain, (1, ATTN_HEADS)) * (HEAD_DIM ** 0.5)),
        "pool_w": _block_diag(pool_w).astype(BF16),
        "pool_scale": row(pool_scale),
        "lru_conv_w": lru_conv_w.astype(F32),
        "lru_conv_b": row(lru_conv_b),
        "lru_wg": (0.5 * jnp.concatenate([_slab_diag(lru_wa), _slab_diag(lru_wx)], axis=-1)).astype(BF16),
        "lru_bg": 0.5 * jnp.concatenate([lru_ba.reshape(depth, -1, 1, LANES), lru_bx.reshape(depth, -1, 1, LANES)],
                                        axis=-1).astype(F32),
        "lru_lambda": row(lru_lambda),
        "w_out": w_out.astype(BF16),
        "norm_mem": row(norm_mem),
        "norm_memkv": row(norm_memkv),
        "w_q_mem": w_q_mem.astype(BF16),
        "w_kv_mem": w_kv_mem.astype(BF16),
        "mq_gain": row(mq_gain * (MEM_HEAD_DIM ** -0.5)),
        "mk_gain": row(mk_gain),
        "w_o_mem": w_o_mem.astype(BF16),
        "norm_ffn": row(norm_ffn),
        "w_up": w_up.astype(BF16),
        "ffn_conv_w": ffn_conv_w.astype(F32),
        "ffn_conv_b": row(ffn_conv_b),
        "w_down": w_down.astype(BF16),
    }
    bias = _attn_bias()
    kmem, vmem = _memkv_call(mem, p)
    h = x
    for l in range(depth):
        q, k, v, pool, lru = _proj_call(h, l, p)
        attn = _attn_call(q, k, v, bias)
        h = _mixmem_call(pool, attn, lru, h, kmem, vmem, l, p)
        h = _ffn_call(h, l, p)
    return h
```

```python
import numpy as np
import jax
import jax.numpy as jnp
from jax import lax
from jax.experimental import pallas as pl
from jax.experimental.pallas import tpu as pltpu

F32 = jnp.float32
BF16 = jnp.bfloat16

LANES = 128
SUBLANES = 8
VMEM_LIMIT_BYTES = 56 * 1024 * 1024

D_MODEL = 1024
POOL_WIDTH = 256
POOL_GROUPS = 4
POOL_WINDOWS = (2, 4, 8, 16)
POOL_HALO = 16
HEAD_DIM = 64
ATTN_WIDTH = 384
ATTN_HEADS = 6
ATTN_PAIRS = ATTN_WIDTH // LANES
DILATIONS = (16, 4, 1)
ATTN_BLOCK = 128
ATTN_UNROLL = 8
LRU_WIDTH = 384
LRU_BLOCKS = 6
LRU_CONV = 4
LRU_C = 8.0
IN_WIDTH = POOL_WIDTH + 3 * ATTN_WIDTH + 2 * LRU_WIDTH
MEM_HEADS = 4
MEM_HEAD_DIM = D_MODEL // MEM_HEADS
D_FF = 2816
FFN_CONV = 3
FFN_CHUNK = 256
EPS = 1e-6
F32_TINY = 1.1754944e-38
NEG = -1e30
LOG2E = 1.4426950408889634

TM_PROJ = 512
TM_MEM = 1024
TM_FFN = 1024

_NT = (((1,), (1,)), ((), ()))


def _rms(x, g):
    ms = jnp.mean(x * x, axis=-1, keepdims=True)
    return x * lax.rsqrt(ms + EPS) * g


def _gelu(x):
    c = 0.7978845608028654
    hx = 0.5 * x
    return hx + hx * jnp.tanh(x * (c + (c * 0.044715) * (x * x)))


def _dot(a, b):
    return jnp.dot(a, b, preferred_element_type=F32)


def _proj_kernel(h_ref, gmix_ref, win_ref, qg_ref, kg_ref, poolw_ref, pscale_ref,
                 cw_ref, cb_ref, wg_ref, bg_ref, lam_ref,
                 q_ref, k_ref, v_ref, pool_ref, lru_ref,
                 uext, xext, hcar, hbuf):
    s = pl.program_id(1)
    tm = h_ref.shape[0]

    @pl.when(s == 0)
    def _():
        uext[0:POOL_HALO, :] = jnp.zeros((POOL_HALO, POOL_WIDTH), F32)
        xext[0:SUBLANES, :] = jnp.zeros((SUBLANES, LRU_WIDTH), F32)
        hcar[...] = jnp.zeros(hcar.shape, F32)

    hn = _rms(h_ref[...], gmix_ref[...]).astype(BF16)
    proj = _dot(hn, win_ref[...])
    o = POOL_WIDTH
    u = proj[:, 0:o]
    q = proj[:, o:o + ATTN_WIDTH]
    k = proj[:, o + ATTN_WIDTH:o + 2 * ATTN_WIDTH]
    v = proj[:, o + 2 * ATTN_WIDTH:o + 3 * ATTN_WIDTH]
    x = proj[:, o + 3 * ATTN_WIDTH:o + 3 * ATTN_WIDTH + LRU_WIDTH]
    y = proj[:, o + 3 * ATTN_WIDTH + LRU_WIDTH:]

    low_head = lax.broadcasted_iota(jnp.int32, (tm, LANES), 1) < HEAD_DIM

    def headnorm(t, gain):
        t2 = t * t
        parts = []
        for c in range(ATTN_PAIRS):
            blk = t2[:, c * LANES:(c + 1) * LANES]
            even = jnp.sum(jnp.where(low_head, blk, 0.0), axis=-1, keepdims=True)
            odd = jnp.sum(jnp.where(low_head, 0.0, blk), axis=-1, keepdims=True)
            parts.append(jnp.where(low_head, even, odd))
        ssq = jnp.concatenate(parts, axis=1)
        return t * lax.rsqrt(ssq + HEAD_DIM * EPS) * gain

    qn = headnorm(q, qg_ref[...])
    kn = headnorm(k, kg_ref[...])
    for c in range(ATTN_PAIRS):
        sl = slice(c * LANES, (c + 1) * LANES)
        q_ref[c] = qn[:, sl]
        k_ref[c] = kn[:, sl]
        v_ref[c] = v[:, sl]

    uext[POOL_HALO:POOL_HALO + tm, :] = u
    e = uext[...]
    s2 = e + pltpu.roll(e, 1, 0)
    s4 = s2 + pltpu.roll(s2, 2, 0)
    s8 = s4 + pltpu.roll(s4, 4, 0)
    s16 = s8 + pltpu.roll(s8, 8, 0)
    uext[0:POOL_HALO, :] = u[tm - POOL_HALO:tm, :]
    tp1 = (s * tm + 1 + lax.broadcasted_iota(jnp.int32, (tm, LANES), 0)).astype(F32)
    low = lax.broadcasted_iota(jnp.int32, (tm, LANES), 1) < POOL_WIDTH // POOL_GROUPS

    def wmean(sw, col, w):
        return sw[POOL_HALO:, col * LANES:(col + 1) * LANES] / jnp.minimum(tp1, float(w))

    pooled = jnp.concatenate(
        [jnp.where(low, wmean(s2, 0, POOL_WINDOWS[0]), wmean(s4, 0, POOL_WINDOWS[1])) - u[:, 0:LANES],
         jnp.where(low, wmean(s8, 1, POOL_WINDOWS[2]), wmean(s16, 1, POOL_WINDOWS[3])) - u[:, LANES:2 * LANES]],
        axis=1)
    pool_ref[...] = (_dot(pooled.astype(BF16), poolw_ref[...]) * pscale_ref[...]).astype(BF16)

    xext[SUBLANES:SUBLANES + tm, :] = x
    ex = xext[...]
    cw = cw_ref[...]
    conv = cb_ref[...] + cw[0:1, :] * ex
    for kk in range(1, LRU_CONV):
        conv = conv + cw[kk:kk + 1, :] * pltpu.roll(ex, kk, 0)
    xc = conv[SUBLANES:, :]
    xext[0:SUBLANES, :] = x[tm - SUBLANES:tm, :]

    xcb = xc.astype(BF16)
    th_a, th_x = [], []
    for c in range(LRU_WIDTH // LANES):
        th = jnp.tanh(_dot(xcb[:, c * LANES:(c + 1) * LANES], wg_ref[c]) + bg_ref[c])
        th_a.append(th[:, 0:LANES])
        th_x.append(th[:, LANES:])
    ig = 0.5 * jnp.concatenate(th_x, axis=1) + 0.5
    z = -lam_ref[...]
    softplus = jnp.maximum(z, 0.0) + jnp.log1p(jnp.exp(-jnp.abs(z)))
    half = (-0.5 * LRU_C) * softplus
    log_a = half * jnp.concatenate(th_a, axis=1) + half
    a = jnp.exp(log_a)
    t = jnp.tanh(-log_a) * (1.0 + a * a)
    b = (t * lax.rsqrt(jnp.maximum(t, F32_TINY))) * (ig * xc)

    ngroups = tm // SUBLANES
    a = a.reshape(ngroups, SUBLANES, LRU_WIDTH)
    b = b.reshape(ngroups, SUBLANES, LRU_WIDTH)
    sub = lax.broadcasted_iota(jnp.int32, (ngroups, SUBLANES, LRU_WIDTH), 1)
    for sh in (1, 2, 4):
        inside = sub >= sh
        a_s = jnp.where(inside, pltpu.roll(a, sh, 1), 1.0)
        b_s = jnp.where(inside, pltpu.roll(b, sh, 1), 0.0)
        b = a * b_s + b
        a = a * a_s
    hp = hcar[0:1, :]
    for j in range(ngroups):
        rows = slice(j * SUBLANES, (j + 1) * SUBLANES)
        hj = a[j] * hp + b[j]
        hbuf[rows, :] = hj
        hp = hj[SUBLANES - 1:SUBLANES, :]
    hcar[...] = jnp.broadcast_to(hp, hcar.shape)
    lru_ref[...] = (hbuf[...] * _gelu(y)).astype(BF16)


def _proj_call(h, l, p):
    B, S, _ = h.shape
    tm = TM_PROJ
    grid = (B, S // tm)
    tile = lambda w: pl.BlockSpec((None, tm, w), lambda b, s: (b, s, 0))
    slab = pl.BlockSpec((None, ATTN_PAIRS, tm, LANES), lambda b, s: (b, 0, s, 0))
    lay = lambda *shape: pl.BlockSpec((None,) + shape, lambda b, s: (l,) + (0,) * len(shape))
    slab_shape = jax.ShapeDtypeStruct((B, ATTN_PAIRS, S, LANES), F32)
    return pl.pallas_call(
        _proj_kernel,
        grid=grid,
        in_specs=[tile(D_MODEL), lay(1, D_MODEL), lay(D_MODEL, IN_WIDTH),
                  lay(1, ATTN_WIDTH), lay(1, ATTN_WIDTH), lay(POOL_WIDTH, POOL_WIDTH), lay(1, POOL_WIDTH),
                  lay(LRU_CONV, LRU_WIDTH), lay(1, LRU_WIDTH), lay(LRU_WIDTH // LANES, LANES, 2 * LANES),
                  lay(LRU_WIDTH // LANES, 1, 2 * LANES), lay(1, LRU_WIDTH)],
        out_specs=[slab, slab, slab, tile(POOL_WIDTH), tile(LRU_WIDTH)],
        out_shape=[slab_shape, slab_shape, slab_shape,
                   jax.ShapeDtypeStruct((B, S, POOL_WIDTH), BF16), jax.ShapeDtypeStruct((B, S, LRU_WIDTH), BF16)],
        scratch_shapes=[pltpu.VMEM((POOL_HALO + tm, POOL_WIDTH), F32), pltpu.VMEM((SUBLANES + tm, LRU_WIDTH), F32),
                        pltpu.VMEM((SUBLANES, LRU_WIDTH), F32), pltpu.VMEM((tm, LRU_WIDTH), F32)],
        compiler_params=pltpu.CompilerParams(dimension_semantics=("arbitrary", "arbitrary"),
                                             vmem_limit_bytes=VMEM_LIMIT_BYTES),
        name="proj",
    )(h, p["norm_mix"], p["w_in"], p["q_gain"], p["k_gain"], p["pool_w"], p["pool_scale"],
      p["lru_conv_w"], p["lru_conv_b"], p["lru_wg"], p["lru_bg"], p["lru_lambda"])


def _attn_kernel(q_ref, k_ref, v_ref, bias_ref, o_ref, qs, ks, vs, oa, ma, la, ob, mb, lb):
    S = o_ref.shape[0]
    nblk = S // ATTN_BLOCK
    Q = ATTN_BLOCK
    low = lax.broadcasted_iota(jnp.int32, (Q, LANES), 1) < HEAD_DIM

    ks[0:Q, :] = jnp.zeros((Q, ATTN_WIDTH), BF16)
    vs[0:Q, :] = jnp.zeros((Q, ATTN_WIDTH), BF16)

    def regroup(src, d_from, step, store):
        Lf, Ln = S // d_from, S // (d_from * step)
        for rf in range(d_from):
            for rs in range(step):
                r_new = rs * d_from + rf
                for c in range(ATTN_PAIRS):
                    rows = pl.ds(rf * Lf + rs, Ln, stride=step) if step > 1 else slice(rf * Lf, (rf + 1) * Lf)
                    store(c, r_new * Ln, Ln, src[c, rows, :])

    def load_qkv(srcs, d_from, step):
        for src, dst, off in zip(srcs, (qs, ks, vs), (0, Q, Q)):
            def store(c, row, n, val, dst=dst, off=off):
                dst[off + row:off + row + n, c * LANES:(c + 1) * LANES] = val.astype(BF16)
            regroup(src, d_from, step, store)

    def reorder_f32(src, dst, d_from, step):
        def store(c, row, n, val):
            dst[c, row:row + n, :] = val
        regroup(src, d_from, step, store)

    def process(pidx, d, src, dst, d_next=None):
        nbl = (S // d) // Q

        def body(g, carry):
            r0 = pl.multiple_of(g * Q, Q)
            if nbl > 1:
                first = jnp.where((g & (nbl - 1)) == 0, 1, 0)
            if dst is not None:
                step = d // d_next
                r, jb = g >> (nbl.bit_length() - 1), g & (nbl - 1)
                out_rows = pl.ds((r & (d_next - 1)) * (S // d_next) + (r >> (d_next.bit_length() - 1))
                                 + jb * (step * Q), Q, stride=step)
            for c in range(ATTN_PAIRS):
                cols = slice(c * LANES, (c + 1) * LANES)
                qb = qs[pl.ds(r0, Q), cols]
                zero = jnp.zeros_like(qb)
                qq = jnp.concatenate([jnp.where(low, qb, zero), jnp.where(low, zero, qb)], axis=0)
                if nbl > 1:
                    kk = ks[pl.ds(r0, 2 * Q), cols]
                    vv = vs[pl.ds(r0, 2 * Q), cols]
                    sc = lax.dot_general(qq, kk, _NT, preferred_element_type=F32) + bias_ref[pidx, first, c]
                else:
                    kk = ks[pl.ds(r0 + Q, Q), cols]
                    vv = vs[pl.ds(r0 + Q, Q), cols]
                    sc = lax.dot_general(qq, kk, _NT, preferred_element_type=F32) + bias_ref[pidx, 1, c, :, Q:2 * Q]
                m = jnp.max(sc, axis=-1, keepdims=True)
                pr = jnp.exp2(sc - m)
                den = jnp.sum(pr, axis=-1, keepdims=True)
                pv = _dot(pr.astype(BF16), vv)
                o_new = jnp.where(low, pv[0:Q, :], pv[Q:2 * Q, :])
                m_new = jnp.where(low, m[0:Q, :], m[Q:2 * Q, :])
                l_new = jnp.where(low, den[0:Q, :], den[Q:2 * Q, :])
                if src is not None:
                    m_old = src[1][c, pl.ds(r0, Q), :]
                    mx = jnp.maximum(m_new, m_old)
                    wn = jnp.exp2(m_new - mx)
                    wo = jnp.exp2(m_old - mx)
                    o_new = wn * o_new + wo * src[0][c, pl.ds(r0, Q), :]
                    l_new = wn * l_new + wo * src[2][c, pl.ds(r0, Q), :]
                    m_new = mx
                if dst is None:
                    o_ref[pl.ds(r0, Q), cols] = (o_new / l_new).astype(BF16)
                else:
                    dst[0][c, out_rows, :] = o_new
                    dst[1][c, out_rows, :] = m_new
                    dst[2][c, out_rows, :] = l_new
            return carry

        lax.fori_loop(0, nblk, body, 0, unroll=ATTN_UNROLL)

    d0, d1, d2 = DILATIONS
    acc_a, acc_b = (oa, ma, la), (ob, mb, lb)
    for src, dst in zip((q_ref, k_ref, v_ref), acc_b):
        reorder_f32(src, dst, d2, d1 // d2)
    load_qkv(acc_b, d1, d0 // d1)
    process(0, d0, None, acc_a, d1)
    load_qkv(acc_b, d1, 1)
    process(1, d1, acc_a, acc_b, d2)
    load_qkv((q_ref, k_ref, v_ref), d2, 1)
    process(2, d2, acc_b, None)


def _attn_bias():
    Q = ATTN_BLOCK
    dist = (Q + np.arange(Q)[:, None] - np.arange(2 * Q)[None, :])
    valid = (dist >= 0) & (dist <= Q)
    has_prev = np.arange(2 * Q)[None, :] >= Q
    slopes = np.asarray([2.0 ** (-8.0 * (h + 1) / ATTN_HEADS) for h in range(ATTN_HEADS)], dtype=np.float32)
    out = np.empty((len(DILATIONS), 2, ATTN_PAIRS, 2 * Q, 2 * Q), np.float32)
    for pi, d in enumerate(DILATIONS):
        for h in range(ATTN_HEADS):
            step = slopes[h] * np.float32(d)
            pen = (step * dist.astype(np.float32)).astype(np.float64) * LOG2E
            rows = slice((h % 2) * Q, (h % 2 + 1) * Q)
            out[pi, 0, h // 2, rows, :] = np.where(valid, -pen, NEG)
            out[pi, 1, h // 2, rows, :] = np.where(valid & has_prev, -pen, NEG)
    return jnp.asarray(out)


def _attn_call(q, k, v, bias):
    B, _, S, _ = q.shape
    slab = pl.BlockSpec((None, ATTN_PAIRS, S, LANES), lambda b: (b, 0, 0, 0))
    acc = pltpu.VMEM((ATTN_PAIRS, S, LANES), F32)
    return pl.pallas_call(
        _attn_kernel,
        grid=(B,),
        in_specs=[slab, slab, slab,
                  pl.BlockSpec(bias.shape, lambda b: (0, 0, 0, 0, 0), pipeline_mode=pl.Buffered(1))],
        out_specs=pl.BlockSpec((None, S, ATTN_WIDTH), lambda b: (b, 0, 0)),
        out_shape=jax.ShapeDtypeStruct((B, S, ATTN_WIDTH), BF16),
        scratch_shapes=[pltpu.VMEM((S, ATTN_WIDTH), BF16), pltpu.VMEM((S + ATTN_BLOCK, ATTN_WIDTH), BF16),
                        pltpu.VMEM((S + ATTN_BLOCK, ATTN_WIDTH), BF16), acc, acc, acc, acc, acc, acc],
        compiler_params=pltpu.CompilerParams(dimension_semantics=("arbitrary",),
                                             vmem_limit_bytes=VMEM_LIMIT_BYTES),
        name="attn",
    )(q, k, v, bias)


def _memkv_kernel(mem_ref, g_ref, w_ref, kg_ref, k_ref, v_ref):
    mn = _rms(mem_ref[...], g_ref[...]).astype(BF16)
    kv = _dot(mn, w_ref[...])
    for hh in range(MEM_HEADS):
        cols = slice(hh * MEM_HEAD_DIM, (hh + 1) * MEM_HEAD_DIM)
        k_ref[:, cols] = _rms(kv[:, cols], kg_ref[...]).astype(BF16)
    v_ref[...] = kv[:, D_MODEL:].astype(BF16)


def _memkv_call(mem, p):
    B, M, _ = mem.shape
    L = p["w_kv_mem"].shape[0]
    lay = lambda *shape: pl.BlockSpec((None,) + shape, lambda l, b: (l,) + (0,) * len(shape))
    out = pl.BlockSpec((None, None, M, D_MODEL), lambda l, b: (l, b, 0, 0))
    shape = jax.ShapeDtypeStruct((L, B, M, D_MODEL), BF16)
    return pl.pallas_call(
        _memkv_kernel,
        grid=(L, B),
        in_specs=[pl.BlockSpec((None, M, D_MODEL), lambda l, b: (b, 0, 0)), lay(1, D_MODEL),
                  lay(D_MODEL, 2 * D_MODEL), lay(1, MEM_HEAD_DIM)],
        out_specs=[out, out],
        out_shape=[shape, shape],
        compiler_params=pltpu.CompilerParams(dimension_semantics=("arbitrary", "arbitrary"),
                                             vmem_limit_bytes=VMEM_LIMIT_BYTES),
        name="memkv",
    )(mem, p["norm_memkv"], p["w_kv_mem"], p["mk_gain"])


def _mixmem_kernel(pool_ref, attn_ref, lru_ref, h_ref, wout_ref, g_ref, wq_ref, qg_ref, k_ref, v_ref, wo_ref,
                   o_ref, mix, obuf):
    mix[:, 0:POOL_WIDTH] = pool_ref[...]
    mix[:, POOL_WIDTH:POOL_WIDTH + ATTN_WIDTH] = attn_ref[...]
    mix[:, POOL_WIDTH + ATTN_WIDTH:] = lru_ref[...]
    h = h_ref[...] + _dot(mix[...], wout_ref[...])
    q = _dot(_rms(h, g_ref[...]).astype(BF16), wq_ref[...])
    for hh in range(MEM_HEADS):
        cols = slice(hh * MEM_HEAD_DIM, (hh + 1) * MEM_HEAD_DIM)
        qh = _rms(q[:, cols], qg_ref[...]).astype(BF16)
        sc = lax.dot_general(qh, k_ref[:, cols], _NT, preferred_element_type=F32)
        m = jnp.max(sc, axis=-1, keepdims=True)
        pr = jnp.exp(sc - m)
        pr = pr / jnp.sum(pr, axis=-1, keepdims=True)
        obuf[:, cols] = _dot(pr.astype(BF16), v_ref[:, cols]).astype(BF16)
    o_ref[...] = h + _dot(obuf[...], wo_ref[...])


def _mixmem_call(pool, attn, lru, h, kmem, vmem, l, p):
    B, S, _ = h.shape
    M = kmem.shape[2]
    tm = TM_MEM
    tile = lambda w: pl.BlockSpec((None, tm, w), lambda b, s: (b, s, 0))
    lay = lambda *shape: pl.BlockSpec((None,) + shape, lambda b, s: (l,) + (0,) * len(shape),
                                      pipeline_mode=pl.Buffered(1))
    kv = pl.BlockSpec((None, None, M, D_MODEL), lambda b, s: (l, b, 0, 0))
    return pl.pallas_call(
        _mixmem_kernel,
        grid=(B, S // tm),
        in_specs=[tile(POOL_WIDTH), tile(ATTN_WIDTH), tile(LRU_WIDTH), tile(D_MODEL), lay(D_MODEL, D_MODEL),
                  lay(1, D_MODEL), lay(D_MODEL, D_MODEL), lay(1, MEM_HEAD_DIM), kv, kv, lay(D_MODEL, D_MODEL)],
        out_specs=tile(D_MODEL),
        out_shape=jax.ShapeDtypeStruct(h.shape, F32),
        scratch_shapes=[pltpu.VMEM((tm, D_MODEL), BF16), pltpu.VMEM((tm, D_MODEL), BF16)],
        compiler_params=pltpu.CompilerParams(dimension_semantics=("arbitrary", "arbitrary"),
                                             vmem_limit_bytes=VMEM_LIMIT_BYTES),
        name="mixmem",
    )(pool, attn, lru, h, p["w_out"], p["norm_mem"], p["w_q_mem"], p["mq_gain"], kmem, vmem, p["w_o_mem"])


def _ffn_kernel(h_ref, g_ref, wup_ref, cw_ref, cb_ref, wdn_ref, o_ref, gcar, act):
    s = pl.program_id(1)
    tm = h_ref.shape[0]

    @pl.when(s == 0)
    def _():
        gcar[...] = jnp.zeros(gcar.shape, F32)

    h = h_ref[...]
    hn = _rms(h, g_ref[...]).astype(BF16)
    for c in range(D_FF // FFN_CHUNK):
        cols = slice(c * FFN_CHUNK, (c + 1) * FFN_CHUNK)
        g = _dot(hn, wup_ref[:, cols])
        u = _dot(hn, wup_ref[:, D_FF + c * FFN_CHUNK:D_FF + (c + 1) * FFN_CHUNK])
        ge = jnp.concatenate([gcar[:, cols], g], axis=0)
        conv = cb_ref[:, cols] + cw_ref[0:1, cols] * ge
        for kk in range(1, FFN_CONV):
            conv = conv + cw_ref[kk:kk + 1, cols] * pltpu.roll(ge, kk, 0)
        gcar[:, cols] = g[tm - SUBLANES:tm, :]
        act[:, cols] = (_gelu(conv[SUBLANES:, :]) * u).astype(BF16)
    o_ref[...] = h + _dot(act[...], wdn_ref[...])


def _ffn_call(h, l, p):
    B, S, _ = h.shape
    tm = TM_FFN
    tile = pl.BlockSpec((None, tm, D_MODEL), lambda b, s: (b, s, 0))
    lay = lambda *shape: pl.BlockSpec((None,) + shape, lambda b, s: (l,) + (0,) * len(shape),
                                      pipeline_mode=pl.Buffered(1))
    return pl.pallas_call(
        _ffn_kernel,
        grid=(B, S // tm),
        in_specs=[tile, lay(1, D_MODEL), lay(D_MODEL, 2 * D_FF), lay(FFN_CONV, D_FF), lay(1, D_FF),
                  lay(D_FF, D_MODEL)],
        out_specs=tile,
        out_shape=jax.ShapeDtypeStruct(h.shape, F32),
        scratch_shapes=[pltpu.VMEM((SUBLANES, D_FF), F32), pltpu.VMEM((tm, D_FF), BF16)],
        compiler_params=pltpu.CompilerParams(dimension_semantics=("arbitrary", "arbitrary"),
                                             vmem_limit_bytes=VMEM_LIMIT_BYTES),
        name="ffn",
    )(h, p["norm_ffn"], p["w_up"], p["ffn_conv_w"], p["ffn_conv_b"], p["w_down"])


def _block_diag(w):
    L, G, n, _ = w.shape
    eye = jnp.eye(G, dtype=w.dtype)
    return jnp.einsum("lgij,gh->lgihj", w, eye).reshape(L, G * n, G * n)


def _slab_diag(w):
    L, G, n, _ = w.shape
    per = LANES // n
    return _block_diag(w.reshape(L * (G // per), per, n, n)).reshape(L, G // per, LANES, LANES)


def kernel(x, mem, norm_mix, w_in, pool_w, pool_scale, q_gain, k_gain, lru_conv_w, lru_conv_b, lru_wa, lru_ba,
           lru_wx, lru_bx, lru_lambda, w_out, norm_mem, norm_memkv, w_q_mem, w_kv_mem, mq_gain, mk_gain, w_o_mem,
           norm_ffn, w_up, ffn_conv_w, ffn_conv_b, w_down):
    depth = w_in.shape[0]
    row = lambda a: a.astype(F32)[:, None, :]
    p = {
        "norm_mix": row(norm_mix),
        "w_in": w_in.astype(BF16),
        "q_gain": row(jnp.tile(q_gain, (1, ATTN_HEADS)) * LOG2E),
        "k_gain": row(jnp.tile(k_gain, (1, ATTN_HEADS)) * (HEAD_DIM ** 0.5)),
        "pool_w": _block_diag(pool_w).astype(BF16),
        "pool_scale": row(pool_scale),
        "lru_conv_w": lru_conv_w.astype(F32),
        "lru_conv_b": row(lru_conv_b),
        "lru_wg": (0.5 * jnp.concatenate([_slab_diag(lru_wa), _slab_diag(lru_wx)], axis=-1)).astype(BF16),
        "lru_bg": 0.5 * jnp.concatenate([lru_ba.reshape(depth, -1, 1, LANES), lru_bx.reshape(depth, -1, 1, LANES)],
                                        axis=-1).astype(F32),
        "lru_lambda": row(lru_lambda),
        "w_out": w_out.astype(BF16),
        "norm_mem": row(norm_mem),
        "norm_memkv": row(norm_memkv),
        "w_q_mem": w_q_mem.astype(BF16),
        "w_kv_mem": w_kv_mem.astype(BF16),
        "mq_gain": row(mq_gain * (MEM_HEAD_DIM ** -0.5)),
        "mk_gain": row(mk_gain),
        "w_o_mem": w_o_mem.astype(BF16),
        "norm_ffn": row(norm_ffn),
        "w_up": w_up.astype(BF16),
        "ffn_conv_w": ffn_conv_w.astype(F32),
        "ffn_conv_b": row(ffn_conv_b),
        "w_down": w_down.astype(BF16),
    }
    bias = _attn_bias()
    kmem, vmem = _memkv_call(mem, p)
    h = x
    for l in range(depth):
        q, k, v, pool, lru = _proj_call(h, l, p)
        attn = _attn_call(q, k, v, bias)
        h = _mixmem_call(pool, attn, lru, h, kmem, vmem, l, p)
        h = _ffn_call(h, l, p)
    return h
```

```python
import numpy as np
import jax
import jax.numpy as jnp
from jax import lax
from jax.experimental import pallas as pl
from jax.experimental.pallas import tpu as pltpu

F32 = jnp.float32
BF16 = jnp.bfloat16

LANES = 128
SUBLANES = 8
VMEM_LIMIT_BYTES = 56 * 1024 * 1024

D_MODEL = 1024
POOL_WIDTH = 256
POOL_GROUPS = 4
POOL_WINDOWS = (2, 4, 8, 16)
POOL_HALO = 16
HEAD_DIM = 64
ATTN_WIDTH = 384
ATTN_HEADS = 6
ATTN_PAIRS = ATTN_WIDTH // LANES
DILATIONS = (16, 4, 1)
ATTN_BLOCK = 128
ATTN_UNROLL = 8
LRU_WIDTH = 384
LRU_BLOCKS = 6
LRU_CONV = 4
LRU_C = 8.0
IN_WIDTH = POOL_WIDTH + 3 * ATTN_WIDTH + 2 * LRU_WIDTH
MEM_HEADS = 4
MEM_HEAD_DIM = D_MODEL // MEM_HEADS
D_FF = 2816
FFN_CONV = 3
FFN_CHUNK = 256
EPS = 1e-6
F32_TINY = 1.1754944e-38
NEG = -1e30
LOG2E = 1.4426950408889634

TM_PROJ = 1024
TM_MEM = 1024
TM_FFN = 1024
TM_MEMKV = 1024

_NT = (((1,), (1,)), ((), ()))


def _rms(x, g):
    ms = jnp.mean(x * x, axis=-1, keepdims=True)
    return x * lax.rsqrt(ms + EPS) * g


def _gelu(x):
    c = 0.7978845608028654
    hx = 0.5 * x
    return hx + hx * jnp.tanh(x * (c + (c * 0.044715) * (x * x)))


def _dot(a, b):
    return jnp.dot(a, b, preferred_element_type=F32)


def _proj_kernel(h_ref, gmix_ref, win_ref, qg_ref, kg_ref, poolw_ref, pscale_ref,
                 cw_ref, cb_ref, wg_ref, bg_ref, lam_ref,
                 q_ref, k_ref, v_ref, pool_ref, lru_ref,
                 uext, xext, hcar, hbuf):
    s = pl.program_id(1)
    tm = h_ref.shape[0]

    @pl.when(s == 0)
    def _():
        uext[0:POOL_HALO, :] = jnp.zeros((POOL_HALO, POOL_WIDTH), F32)
        xext[0:SUBLANES, :] = jnp.zeros((SUBLANES, LRU_WIDTH), F32)
        hcar[...] = jnp.zeros(hcar.shape, F32)

    hn = _rms(h_ref[...], gmix_ref[...]).astype(BF16)
    proj = _dot(hn, win_ref[...])
    o = POOL_WIDTH
    u = proj[:, 0:o]
    q = proj[:, o:o + ATTN_WIDTH]
    k = proj[:, o + ATTN_WIDTH:o + 2 * ATTN_WIDTH]
    v = proj[:, o + 2 * ATTN_WIDTH:o + 3 * ATTN_WIDTH]
    x = proj[:, o + 3 * ATTN_WIDTH:o + 3 * ATTN_WIDTH + LRU_WIDTH]
    y = proj[:, o + 3 * ATTN_WIDTH + LRU_WIDTH:]

    low_head = lax.broadcasted_iota(jnp.int32, (tm, LANES), 1) < HEAD_DIM

    def headnorm(t, gain):
        t2 = t * t
        parts = []
        for c in range(ATTN_PAIRS):
            blk = t2[:, c * LANES:(c + 1) * LANES]
            even = jnp.sum(jnp.where(low_head, blk, 0.0), axis=-1, keepdims=True)
            odd = jnp.sum(jnp.where(low_head, 0.0, blk), axis=-1, keepdims=True)
            parts.append(jnp.where(low_head, even, odd))
        ssq = jnp.concatenate(parts, axis=1)
        return t * lax.rsqrt(ssq + HEAD_DIM * EPS) * gain

    qn = headnorm(q, qg_ref[...])
    kn = headnorm(k, kg_ref[...])
    for c in range(ATTN_PAIRS):
        sl = slice(c * LANES, (c + 1) * LANES)
        q_ref[c] = qn[:, sl]
        k_ref[c] = kn[:, sl]
        v_ref[c] = v[:, sl]

    uext[POOL_HALO:POOL_HALO + tm, :] = u
    e = uext[...]
    s2 = e + pltpu.roll(e, 1, 0)
    s4 = s2 + pltpu.roll(s2, 2, 0)
    s8 = s4 + pltpu.roll(s4, 4, 0)
    s16 = s8 + pltpu.roll(s8, 8, 0)
    uext[0:POOL_HALO, :] = u[tm - POOL_HALO:tm, :]
    tp1 = (s * tm + 1 + lax.broadcasted_iota(jnp.int32, (tm, LANES), 0)).astype(F32)
    low = lax.broadcasted_iota(jnp.int32, (tm, LANES), 1) < POOL_WIDTH // POOL_GROUPS

    def wmean(sw, col, w):
        return sw[POOL_HALO:, col * LANES:(col + 1) * LANES] / jnp.minimum(tp1, float(w))

    pooled = jnp.concatenate(
        [jnp.where(low, wmean(s2, 0, POOL_WINDOWS[0]), wmean(s4, 0, POOL_WINDOWS[1])) - u[:, 0:LANES],
         jnp.where(low, wmean(s8, 1, POOL_WINDOWS[2]), wmean(s16, 1, POOL_WINDOWS[3])) - u[:, LANES:2 * LANES]],
        axis=1)
    pool_ref[...] = (_dot(pooled.astype(BF16), poolw_ref[...]) * pscale_ref[...]).astype(BF16)

    xext[SUBLANES:SUBLANES + tm, :] = x
    ex = xext[...]
    cw = cw_ref[...]
    conv = cb_ref[...] + cw[0:1, :] * ex
    for kk in range(1, LRU_CONV):
        conv = conv + cw[kk:kk + 1, :] * pltpu.roll(ex, kk, 0)
    xc = conv[SUBLANES:, :]
    xext[0:SUBLANES, :] = x[tm - SUBLANES:tm, :]

    xcb = xc.astype(BF16)
    th_a, th_x = [], []
    for c in range(LRU_WIDTH // LANES):
        th = jnp.tanh(_dot(xcb[:, c * LANES:(c + 1) * LANES], wg_ref[c]) + bg_ref[c])
        th_a.append(th[:, 0:LANES])
        th_x.append(th[:, LANES:])
    ig = 0.5 * jnp.concatenate(th_x, axis=1) + 0.5
    z = -lam_ref[...]
    softplus = jnp.maximum(z, 0.0) + jnp.log1p(jnp.exp(-jnp.abs(z)))
    half = (-0.5 * LRU_C) * softplus
    log_a = half * jnp.concatenate(th_a, axis=1) + half
    a = jnp.exp(log_a)
    t = jnp.tanh(-log_a) * (1.0 + a * a)
    b = (t * lax.rsqrt(jnp.maximum(t, F32_TINY))) * (ig * xc)

    ngroups = tm // SUBLANES
    a = a.reshape(ngroups, SUBLANES, LRU_WIDTH)
    b = b.reshape(ngroups, SUBLANES, LRU_WIDTH)
    sub = lax.broadcasted_iota(jnp.int32, (ngroups, SUBLANES, LRU_WIDTH), 1)
    for sh in (1, 2, 4):
        inside = sub >= sh
        a_s = jnp.where(inside, pltpu.roll(a, sh, 1), 1.0)
        b_s = jnp.where(inside, pltpu.roll(b, sh, 1), 0.0)
        b = a * b_s + b
        a = a * a_s
    hp = hcar[0:1, :]
    for j in range(ngroups):
        rows = slice(j * SUBLANES, (j + 1) * SUBLANES)
        hj = a[j] * hp + b[j]
        hbuf[rows, :] = hj
        hp = hj[SUBLANES - 1:SUBLANES, :]
    hcar[...] = jnp.broadcast_to(hp, hcar.shape)
    lru_ref[...] = (hbuf[...] * _gelu(y)).astype(BF16)


def _proj_call(h, l, p):
    B, S, _ = h.shape
    tm = TM_PROJ
    grid = (B, S // tm)
    tile = lambda w: pl.BlockSpec((None, tm, w), lambda b, s: (b, s, 0))
    slab = pl.BlockSpec((None, ATTN_PAIRS, tm, LANES), lambda b, s: (b, 0, s, 0))
    lay = lambda *shape: pl.BlockSpec((None,) + shape, lambda b, s: (l,) + (0,) * len(shape))
    slab_shape = jax.ShapeDtypeStruct((B, ATTN_PAIRS, S, LANES), F32)
    return pl.pallas_call(
        _proj_kernel,
        grid=grid,
        in_specs=[tile(D_MODEL), lay(1, D_MODEL), lay(D_MODEL, IN_WIDTH),
                  lay(1, ATTN_WIDTH), lay(1, ATTN_WIDTH), lay(POOL_WIDTH, POOL_WIDTH), lay(1, POOL_WIDTH),
                  lay(LRU_CONV, LRU_WIDTH), lay(1, LRU_WIDTH), lay(LRU_WIDTH // LANES, LANES, 2 * LANES),
                  lay(LRU_WIDTH // LANES, 1, 2 * LANES), lay(1, LRU_WIDTH)],
        out_specs=[slab, slab, slab, tile(POOL_WIDTH), tile(LRU_WIDTH)],
        out_shape=[slab_shape, slab_shape, slab_shape,
                   jax.ShapeDtypeStruct((B, S, POOL_WIDTH), BF16), jax.ShapeDtypeStruct((B, S, LRU_WIDTH), BF16)],
        scratch_shapes=[pltpu.VMEM((POOL_HALO + tm, POOL_WIDTH), F32), pltpu.VMEM((SUBLANES + tm, LRU_WIDTH), F32),
                        pltpu.VMEM((SUBLANES, LRU_WIDTH), F32), pltpu.VMEM((tm, LRU_WIDTH), F32)],
        compiler_params=pltpu.CompilerParams(dimension_semantics=("arbitrary", "arbitrary"),
                                             vmem_limit_bytes=VMEM_LIMIT_BYTES),
        name="proj",
    )(h, p["norm_mix"], p["w_in"], p["q_gain"], p["k_gain"], p["pool_w"], p["pool_scale"],
      p["lru_conv_w"], p["lru_conv_b"], p["lru_wg"], p["lru_bg"], p["lru_lambda"])


def _attn_kernel(q_ref, k_ref, v_ref, bias_ref, o_ref, qs, ks, vs, oa, ma, la, ob, mb, lb):
    S = o_ref.shape[0]
    nblk = S // ATTN_BLOCK
    Q = ATTN_BLOCK
    low = lax.broadcasted_iota(jnp.int32, (Q, LANES), 1) < HEAD_DIM

    ks[0:Q, :] = jnp.zeros((Q, ATTN_WIDTH), BF16)
    vs[0:Q, :] = jnp.zeros((Q, ATTN_WIDTH), BF16)

    def regroup(src, d_from, step, store):
        Lf, Ln = S // d_from, S // (d_from * step)
        for rf in range(d_from):
            for rs in range(step):
                r_new = rs * d_from + rf
                for c in range(ATTN_PAIRS):
                    rows = pl.ds(rf * Lf + rs, Ln, stride=step) if step > 1 else slice(rf * Lf, (rf + 1) * Lf)
                    store(c, r_new * Ln, Ln, src[c, rows, :])

    def load_qkv(srcs, d_from, step):
        for src, dst, off in zip(srcs, (qs, ks, vs), (0, Q, Q)):
            def store(c, row, n, val, dst=dst, off=off):
                dst[off + row:off + row + n, c * LANES:(c + 1) * LANES] = val.astype(BF16)
            regroup(src, d_from, step, store)

    def reorder_f32(src, dst, d_from, step):
        def store(c, row, n, val):
            dst[c, row:row + n, :] = val
        regroup(src, d_from, step, store)

    def process(pidx, d, src, dst, d_next=None):
        nbl = (S // d) // Q

        def body(g, carry):
            r0 = pl.multiple_of(g * Q, Q)
            if nbl > 1:
                first = jnp.where((g & (nbl - 1)) == 0, 1, 0)
            if dst is not None:
                step = d // d_next
                r, jb = g >> (nbl.bit_length() - 1), g & (nbl - 1)
                out_rows = pl.ds((r & (d_next - 1)) * (S // d_next) + (r >> (d_next.bit_length() - 1))
                                 + jb * (step * Q), Q, stride=step)
            for c in range(ATTN_PAIRS):
                cols = slice(c * LANES, (c + 1) * LANES)
                qb = qs[pl.ds(r0, Q), cols]
                zero = jnp.zeros_like(qb)
                qq = jnp.concatenate([jnp.where(low, qb, zero), jnp.where(low, zero, qb)], axis=0)
                if nbl > 1:
                    kk = ks[pl.ds(r0, 2 * Q), cols]
                    vv = vs[pl.ds(r0, 2 * Q), cols]
                    sc = lax.dot_general(qq, kk, _NT, preferred_element_type=F32) + bias_ref[pidx, first, c]
                else:
                    kk = ks[pl.ds(r0 + Q, Q), cols]
                    vv = vs[pl.ds(r0 + Q, Q), cols]
                    sc = lax.dot_general(qq, kk, _NT, preferred_element_type=F32) + bias_ref[pidx, 1, c, :, Q:2 * Q]
                m = jnp.max(sc, axis=-1, keepdims=True)
                pr = jnp.exp2(sc - m)
                den = jnp.sum(pr, axis=-1, keepdims=True)
                pv = _dot(pr.astype(BF16), vv)
                o_new = jnp.where(low, pv[0:Q, :], pv[Q:2 * Q, :])
                m_new = jnp.where(low, m[0:Q, :], m[Q:2 * Q, :])
                l_new = jnp.where(low, den[0:Q, :], den[Q:2 * Q, :])
                if src is not None:
                    m_old = src[1][c, pl.ds(r0, Q), :]
                    mx = jnp.maximum(m_new, m_old)
                    wn = jnp.exp2(m_new - mx)
                    wo = jnp.exp2(m_old - mx)
                    o_new = wn * o_new + wo * src[0][c, pl.ds(r0, Q), :]
                    l_new = wn * l_new + wo * src[2][c, pl.ds(r0, Q), :]
                    m_new = mx
                if dst is None:
                    o_ref[pl.ds(r0, Q), cols] = (o_new / l_new).astype(BF16)
                else:
                    dst[0][c, out_rows, :] = o_new
                    dst[1][c, out_rows, :] = m_new
                    dst[2][c, out_rows, :] = l_new
            return carry

        lax.fori_loop(0, nblk, body, 0, unroll=ATTN_UNROLL if nbl > 1 else 2 * ATTN_UNROLL)

    d0, d1, d2 = DILATIONS
    acc_a, acc_b = (oa, ma, la), (ob, mb, lb)
    for src, dst in zip((q_ref, k_ref, v_ref), acc_b):
        reorder_f32(src, dst, d2, d1 // d2)
    load_qkv(acc_b, d1, d0 // d1)
    process(0, d0, None, acc_a, d1)
    load_qkv(acc_b, d1, 1)
    process(1, d1, acc_a, acc_b, d2)
    load_qkv((q_ref, k_ref, v_ref), d2, 1)
    process(2, d2, acc_b, None)


def _attn_bias():
    Q = ATTN_BLOCK
    dist = (Q + np.arange(Q)[:, None] - np.arange(2 * Q)[None, :])
    valid = (dist >= 0) & (dist <= Q)
    has_prev = np.arange(2 * Q)[None, :] >= Q
    slopes = np.asarray([2.0 ** (-8.0 * (h + 1) / ATTN_HEADS) for h in range(ATTN_HEADS)], dtype=np.float32)
    out = np.empty((len(DILATIONS), 2, ATTN_PAIRS, 2 * Q, 2 * Q), np.float32)
    for pi, d in enumerate(DILATIONS):
        for h in range(ATTN_HEADS):
            step = slopes[h] * np.float32(d)
            pen = (step * dist.astype(np.float32)).astype(np.float64) * LOG2E
            rows = slice((h % 2) * Q, (h % 2 + 1) * Q)
            out[pi, 0, h // 2, rows, :] = np.where(valid, -pen, NEG)
            out[pi, 1, h // 2, rows, :] = np.where(valid & has_prev, -pen, NEG)
    return jnp.asarray(out)


def _attn_call(q, k, v, bias):
    B, _, S, _ = q.shape
    slab = pl.BlockSpec((None, ATTN_PAIRS, S, LANES), lambda b: (b, 0, 0, 0))
    acc = pltpu.VMEM((ATTN_PAIRS, S, LANES), F32)
    return pl.pallas_call(
        _attn_kernel,
        grid=(B,),
        in_specs=[slab, slab, slab,
                  pl.BlockSpec(bias.shape, lambda b: (0, 0, 0, 0, 0), pipeline_mode=pl.Buffered(1))],
        out_specs=pl.BlockSpec((None, S, ATTN_WIDTH), lambda b: (b, 0, 0)),
        out_shape=jax.ShapeDtypeStruct((B, S, ATTN_WIDTH), BF16),
        scratch_shapes=[pltpu.VMEM((S, ATTN_WIDTH), BF16), pltpu.VMEM((S + ATTN_BLOCK, ATTN_WIDTH), BF16),
                        pltpu.VMEM((S + ATTN_BLOCK, ATTN_WIDTH), BF16), acc, acc, acc, acc, acc, acc],
        compiler_params=pltpu.CompilerParams(dimension_semantics=("arbitrary",),
                                             vmem_limit_bytes=VMEM_LIMIT_BYTES),
        name="attn",
    )(q, k, v, bias)


def _memkv_kernel(mem_ref, g_ref, w_ref, kg_ref, k_ref, v_ref):
    mn = _rms(mem_ref[...], g_ref[...]).astype(BF16)
    kv = _dot(mn, w_ref[...])
    for hh in range(MEM_HEADS):
        cols = slice(hh * MEM_HEAD_DIM, (hh + 1) * MEM_HEAD_DIM)
        k_ref[:, cols] = _rms(kv[:, cols], kg_ref[...]).astype(BF16)
    v_ref[...] = kv[:, D_MODEL:].astype(BF16)


def _memkv_call(mem, p):
    B, M, _ = mem.shape
    L = p["w_kv_mem"].shape[0]
    rows = B * M
    tm = min(TM_MEMKV, rows)
    lay = lambda *shape: pl.BlockSpec((None,) + shape, lambda l, t: (l,) + (0,) * len(shape))
    out = pl.BlockSpec((None, tm, D_MODEL), lambda l, t: (l, t, 0))
    shape = jax.ShapeDtypeStruct((L, rows, D_MODEL), BF16)
    k, v = pl.pallas_call(
        _memkv_kernel,
        grid=(L, rows // tm),
        in_specs=[pl.BlockSpec((tm, D_MODEL), lambda l, t: (t, 0)), lay(1, D_MODEL),
                  lay(D_MODEL, 2 * D_MODEL), lay(1, MEM_HEAD_DIM)],
        out_specs=[out, out],
        out_shape=[shape, shape],
        compiler_params=pltpu.CompilerParams(dimension_semantics=("arbitrary", "arbitrary"),
                                             vmem_limit_bytes=VMEM_LIMIT_BYTES),
        name="memkv",
    )(mem.reshape(rows, D_MODEL), p["norm_memkv"], p["w_kv_mem"], p["mk_gain"])
    return k.reshape(L, B, M, D_MODEL), v.reshape(L, B, M, D_MODEL)


def _mixmem_kernel(pool_ref, attn_ref, lru_ref, h_ref, wout_ref, g_ref, wq_ref, qg_ref, k_ref, v_ref, wo_ref,
                   o_ref, mix, obuf):
    mix[:, 0:POOL_WIDTH] = pool_ref[...]
    mix[:, POOL_WIDTH:POOL_WIDTH + ATTN_WIDTH] = attn_ref[...]
    mix[:, POOL_WIDTH + ATTN_WIDTH:] = lru_ref[...]
    h = h_ref[...] + _dot(mix[...], wout_ref[...])
    q = _dot(_rms(h, g_ref[...]).astype(BF16), wq_ref[...])
    for hh in range(MEM_HEADS):
        cols = slice(hh * MEM_HEAD_DIM, (hh + 1) * MEM_HEAD_DIM)
        qh = _rms(q[:, cols], qg_ref[...]).astype(BF16)
        sc = lax.dot_general(qh, k_ref[:, cols], _NT, preferred_element_type=F32)
        m = jnp.max(sc, axis=-1, keepdims=True)
        pr = jnp.exp(sc - m)
        pr = pr / jnp.sum(pr, axis=-1, keepdims=True)
        obuf[:, cols] = _dot(pr.astype(BF16), v_ref[:, cols]).astype(BF16)
    o_ref[...] = h + _dot(obuf[...], wo_ref[...])


def _mixmem_call(pool, attn, lru, h, kmem, vmem, l, p):
    B, S, _ = h.shape
    M = kmem.shape[2]
    tm = TM_MEM
    tile = lambda w: pl.BlockSpec((None, tm, w), lambda b, s: (b, s, 0))
    lay = lambda *shape: pl.BlockSpec((None,) + shape, lambda b, s: (l,) + (0,) * len(shape),
                                      pipeline_mode=pl.Buffered(1))
    kv = pl.BlockSpec((None, None, M, D_MODEL), lambda b, s: (l, b, 0, 0))
    return pl.pallas_call(
        _mixmem_kernel,
        grid=(B, S // tm),
        in_specs=[tile(POOL_WIDTH), tile(ATTN_WIDTH), tile(LRU_WIDTH), tile(D_MODEL), lay(D_MODEL, D_MODEL),
                  lay(1, D_MODEL), lay(D_MODEL, D_MODEL), lay(1, MEM_HEAD_DIM), kv, kv, lay(D_MODEL, D_MODEL)],
        out_specs=tile(D_MODEL),
        out_shape=jax.ShapeDtypeStruct(h.shape, F32),
        scratch_shapes=[pltpu.VMEM((tm, D_MODEL), BF16), pltpu.VMEM((tm, D_MODEL), BF16)],
        compiler_params=pltpu.CompilerParams(dimension_semantics=("arbitrary", "arbitrary"),
                                             vmem_limit_bytes=VMEM_LIMIT_BYTES),
        name="mixmem",
    )(pool, attn, lru, h, p["w_out"], p["norm_mem"], p["w_q_mem"], p["mq_gain"], kmem, vmem, p["w_o_mem"])


def _ffn_kernel(h_ref, g_ref, wup_ref, cw_ref, cb_ref, wdn_ref, o_ref, gcar, act):
    s = pl.program_id(1)
    tm = h_ref.shape[0]

    @pl.when(s == 0)
    def _():
        gcar[...] = jnp.zeros(gcar.shape, F32)

    h = h_ref[...]
    hn = _rms(h, g_ref[...]).astype(BF16)
    for c in range(D_FF // FFN_CHUNK):
        cols = slice(c * FFN_CHUNK, (c + 1) * FFN_CHUNK)
        g = _dot(hn, wup_ref[:, cols])
        u = _dot(hn, wup_ref[:, D_FF + c * FFN_CHUNK:D_FF + (c + 1) * FFN_CHUNK])
        ge = jnp.concatenate([gcar[:, cols], g], axis=0)
        conv = cb_ref[:, cols] + cw_ref[0:1, cols] * ge
        for kk in range(1, FFN_CONV):
            conv = conv + cw_ref[kk:kk + 1, cols] * pltpu.roll(ge, kk, 0)
        gcar[:, cols] = g[tm - SUBLANES:tm, :]
        act[:, cols] = (_gelu(conv[SUBLANES:, :]) * u).astype(BF16)
    o_ref[...] = h + _dot(act[...], wdn_ref[...])


def _ffn_call(h, l, p):
    B, S, _ = h.shape
    tm = TM_FFN
    tile = pl.BlockSpec((None, tm, D_MODEL), lambda b, s: (b, s, 0))
    lay = lambda *shape: pl.BlockSpec((None,) + shape, lambda b, s: (l,) + (0,) * len(shape),
                                      pipeline_mode=pl.Buffered(1))
    return pl.pallas_call(
        _ffn_kernel,
        grid=(B, S // tm),
        in_specs=[tile, lay(1, D_MODEL), lay(D_MODEL, 2 * D_FF), lay(FFN_CONV, D_FF), lay(1, D_FF),
                  lay(D_FF, D_MODEL)],
        out_specs=tile,
        out_shape=jax.ShapeDtypeStruct(h.shape, F32),
        scratch_shapes=[pltpu.VMEM((SUBLANES, D_FF), F32), pltpu.VMEM((tm, D_FF), BF16)],
        compiler_params=pltpu.CompilerParams(dimension_semantics=("arbitrary", "arbitrary"),
                                             vmem_limit_bytes=VMEM_LIMIT_BYTES),
        name="ffn",
    )(h, p["norm_ffn"], p["w_up"], p["ffn_conv_w"], p["ffn_conv_b"], p["w_down"])


def _block_diag(w):
    L, G, n, _ = w.shape
    eye = jnp.eye(G, dtype=w.dtype)
    return jnp.einsum("lgij,gh->lgihj", w, eye).reshape(L, G * n, G * n)


def _slab_diag(w):
    L, G, n, _ = w.shape
    per = LANES // n
    return _block_diag(w.reshape(L * (G // per), per, n, n)).reshape(L, G // per, LANES, LANES)


def kernel(x, mem, norm_mix, w_in, pool_w, pool_scale, q_gain, k_gain, lru_conv_w, lru_conv_b, lru_wa, lru_ba,
           lru_wx, lru_bx, lru_lambda, w_out, norm_mem, norm_memkv, w_q_mem, w_kv_mem, mq_gain, mk_gain, w_o_mem,
           norm_ffn, w_up, ffn_conv_w, ffn_conv_b, w_down):
    depth = w_in.shape[0]
    row = lambda a: a.astype(F32)[:, None, :]
    p = {
        "norm_mix": row(norm_mix),
        "w_in": w_in.astype(BF16),
        "q_gain": row(jnp.tile(q_gain, (1, ATTN_HEADS)) * LOG2E),
        "k_gain": row(jnp.tile(k_gain, (1, ATTN_HEADS)) * (HEAD_DIM ** 0.5)),
        "pool_w": _block_diag(pool_w).astype(BF16),
        "pool_scale": row(pool_scale),
        "lru_conv_w": lru_conv_w.astype(F32),
        "lru_conv_b": row(lru_conv_b),
        "lru_wg": (0.5 * jnp.concatenate([_slab_diag(lru_wa), _slab_diag(lru_wx)], axis=-1)).astype(BF16),
        "lru_bg": 0.5 * jnp.concatenate([lru_ba.reshape(depth, -1, 1, LANES), lru_bx.reshape(depth, -1, 1, LANES)],
                                        axis=-1).astype(F32),
        "lru_lambda": row(lru_lambda),
        "w_out": w_out.astype(BF16),
        "norm_mem": row(norm_mem),
        "norm_memkv": row(norm_memkv),
        "w_q_mem": w_q_mem.astype(BF16),
        "w_kv_mem": w_kv_mem.astype(BF16),
        "mq_gain": row(mq_gain * (MEM_HEAD_DIM ** -0.5)),
        "mk_gain": row(mk_gain),
        "w_o_mem": w_o_mem.astype(BF16),
        "norm_ffn": row(norm_ffn),
        "w_up": w_up.astype(BF16),
        "ffn_conv_w": ffn_conv_w.astype(F32),
        "ffn_conv_b": row(ffn_conv_b),
        "w_down": w_down.astype(BF16),
    }
    bias = _attn_bias()
    kmem, vmem = _memkv_call(mem, p)
    h = x
    for l in range(depth):
        q, k, v, pool, lru = _proj_call(h, l, p)
        attn = _attn_call(q, k, v, bias)
        h = _mixmem_call(pool, attn, lru, h, kmem, vmem, l, p)
        h = _ffn_call(h, l, p)
    return h
```

```python
import numpy as np
import jax
import jax.numpy as jnp
from jax import lax
from jax.experimental import pallas as pl
from jax.experimental.pallas import tpu as pltpu

F32 = jnp.float32
BF16 = jnp.bfloat16

LANES = 128
SUBLANES = 8
VMEM_LIMIT_BYTES = 56 * 1024 * 1024

D_MODEL = 1024
POOL_WIDTH = 256
POOL_GROUPS = 4
POOL_WINDOWS = (2, 4, 8, 16)
POOL_HALO = 16
HEAD_DIM = 64
ATTN_WIDTH = 384
ATTN_HEADS = 6
ATTN_PAIRS = ATTN_WIDTH // LANES
DILATIONS = (16, 4, 1)
ATTN_BLOCK = 128
ATTN_UNROLL = 8
LRU_WIDTH = 384
LRU_BLOCKS = 6
LRU_CONV = 4
LRU_C = 8.0
IN_WIDTH = POOL_WIDTH + 3 * ATTN_WIDTH + 2 * LRU_WIDTH
MEM_HEADS = 4
MEM_HEAD_DIM = D_MODEL // MEM_HEADS
D_FF = 2816
FFN_CONV = 3
FFN_CHUNK = 256
EPS = 1e-6
F32_TINY = 1.1754944e-38
NEG = -1e30
LOG2E = 1.4426950408889634

TM_PROJ = 1024
TM_MEM = 1024
TM_FFN = 1024
TM_MEMKV = 1024

_NT = (((1,), (1,)), ((), ()))


def _rms(x, g):
    ms = jnp.mean(x * x, axis=-1, keepdims=True)
    return x * lax.rsqrt(ms + EPS) * g


def _gelu(x):
    c = 0.7978845608028654
    hx = 0.5 * x
    return hx + hx * jnp.tanh(x * (c + (c * 0.044715) * (x * x)))


def _dot(a, b):
    return jnp.dot(a, b, preferred_element_type=F32)


def _proj_kernel(h_ref, gmix_ref, win_ref, qg_ref, kg_ref, poolw_ref, pscale_ref,
                 cw_ref, cb_ref, wg_ref, bg_ref, lam_ref,
                 q_ref, k_ref, v_ref, pool_ref, lru_ref,
                 uext, xext, hcar, hbuf):
    s = pl.program_id(1)
    tm = h_ref.shape[0]

    @pl.when(s == 0)
    def _():
        uext[0:POOL_HALO, :] = jnp.zeros((POOL_HALO, POOL_WIDTH), F32)
        xext[0:SUBLANES, :] = jnp.zeros((SUBLANES, LRU_WIDTH), F32)
        hcar[...] = jnp.zeros(hcar.shape, F32)

    hn = _rms(h_ref[...], gmix_ref[...]).astype(BF16)
    proj = _dot(hn, win_ref[...])
    o = 2 * LRU_WIDTH
    x = proj[:, 0:LRU_WIDTH]
    y = proj[:, LRU_WIDTH:o]
    u = proj[:, o:o + POOL_WIDTH]
    q = proj[:, o + POOL_WIDTH:o + POOL_WIDTH + ATTN_WIDTH]
    k = proj[:, o + POOL_WIDTH + ATTN_WIDTH:o + POOL_WIDTH + 2 * ATTN_WIDTH]
    v = proj[:, o + POOL_WIDTH + 2 * ATTN_WIDTH:]

    xext[SUBLANES:SUBLANES + tm, :] = x
    ex = xext[...]
    cw = cw_ref[...]
    conv = cb_ref[...] + cw[0:1, :] * ex
    for kk in range(1, LRU_CONV):
        conv = conv + cw[kk:kk + 1, :] * pltpu.roll(ex, kk, 0)
    xc = conv[SUBLANES:, :]
    xext[0:SUBLANES, :] = x[tm - SUBLANES:tm, :]

    xcb = xc.astype(BF16)
    th_a, th_x = [], []
    for c in range(LRU_WIDTH // LANES):
        th = jnp.tanh(_dot(xcb[:, c * LANES:(c + 1) * LANES], wg_ref[c]) + bg_ref[c])
        th_a.append(th[:, 0:LANES])
        th_x.append(th[:, LANES:])
    ig = 0.5 * jnp.concatenate(th_x, axis=1) + 0.5
    z = -lam_ref[...]
    softplus = jnp.maximum(z, 0.0) + jnp.log1p(jnp.exp(-jnp.abs(z)))
    half = (-0.5 * LRU_C) * softplus
    log_a = half * jnp.concatenate(th_a, axis=1) + half
    a = jnp.exp(log_a)
    t = jnp.tanh(-log_a) * (1.0 + a * a)
    b = (t * lax.rsqrt(jnp.maximum(t, F32_TINY))) * (ig * xc)

    ngroups = tm // SUBLANES
    a = a.reshape(ngroups, SUBLANES, LRU_WIDTH)
    b = b.reshape(ngroups, SUBLANES, LRU_WIDTH)
    sub = lax.broadcasted_iota(jnp.int32, (ngroups, SUBLANES, LRU_WIDTH), 1)
    for sh in (1, 2, 4):
        inside = sub >= sh
        a_s = jnp.where(inside, pltpu.roll(a, sh, 1), 1.0)
        b_s = jnp.where(inside, pltpu.roll(b, sh, 1), 0.0)
        b = a * b_s + b
        a = a * a_s
    hp = hcar[0:1, :]
    for j in range(ngroups):
        rows = slice(j * SUBLANES, (j + 1) * SUBLANES)
        hj = a[j] * hp + b[j]
        hbuf[rows, :] = hj
        hp = hj[SUBLANES - 1:SUBLANES, :]
    hcar[...] = jnp.broadcast_to(hp, hcar.shape)
    lru_ref[...] = (hbuf[...] * _gelu(y)).astype(BF16)

    low_head = lax.broadcasted_iota(jnp.int32, (tm, LANES), 1) < HEAD_DIM

    def headnorm(t, gain):
        t2 = t * t
        parts = []
        for c in range(ATTN_PAIRS):
            blk = t2[:, c * LANES:(c + 1) * LANES]
            even = jnp.sum(jnp.where(low_head, blk, 0.0), axis=-1, keepdims=True)
            odd = jnp.sum(jnp.where(low_head, 0.0, blk), axis=-1, keepdims=True)
            parts.append(jnp.where(low_head, even, odd))
        ssq = jnp.concatenate(parts, axis=1)
        return t * lax.rsqrt(ssq + HEAD_DIM * EPS) * gain

    qn = headnorm(q, qg_ref[...])
    kn = headnorm(k, kg_ref[...])
    for c in range(ATTN_PAIRS):
        sl = slice(c * LANES, (c + 1) * LANES)
        q_ref[c] = qn[:, sl]
        k_ref[c] = kn[:, sl]
        v_ref[c] = v[:, sl]

    uext[POOL_HALO:POOL_HALO + tm, :] = u
    e = uext[...]
    s2 = e + pltpu.roll(e, 1, 0)
    s4 = s2 + pltpu.roll(s2, 2, 0)
    s8 = s4 + pltpu.roll(s4, 4, 0)
    s16 = s8 + pltpu.roll(s8, 8, 0)
    uext[0:POOL_HALO, :] = u[tm - POOL_HALO:tm, :]
    tp1 = (s * tm + 1 + lax.broadcasted_iota(jnp.int32, (tm, LANES), 0)).astype(F32)
    low = lax.broadcasted_iota(jnp.int32, (tm, LANES), 1) < POOL_WIDTH // POOL_GROUPS

    def wmean(sw, col, w):
        return sw[POOL_HALO:, col * LANES:(col + 1) * LANES] / jnp.minimum(tp1, float(w))

    pooled = jnp.concatenate(
        [jnp.where(low, wmean(s2, 0, POOL_WINDOWS[0]), wmean(s4, 0, POOL_WINDOWS[1])) - u[:, 0:LANES],
         jnp.where(low, wmean(s8, 1, POOL_WINDOWS[2]), wmean(s16, 1, POOL_WINDOWS[3])) - u[:, LANES:2 * LANES]],
        axis=1)
    pool_ref[...] = (_dot(pooled.astype(BF16), poolw_ref[...]) * pscale_ref[...]).astype(BF16)


def _proj_call(h, l, p):
    B, S, _ = h.shape
    tm = TM_PROJ
    grid = (B, S // tm)
    tile = lambda w: pl.BlockSpec((None, tm, w), lambda b, s: (b, s, 0))
    slab = pl.BlockSpec((None, ATTN_PAIRS, tm, LANES), lambda b, s: (b, 0, s, 0))
    lay = lambda *shape: pl.BlockSpec((None,) + shape, lambda b, s: (l,) + (0,) * len(shape))
    slab_shape = jax.ShapeDtypeStruct((B, ATTN_PAIRS, S, LANES), F32)
    return pl.pallas_call(
        _proj_kernel,
        grid=grid,
        in_specs=[tile(D_MODEL), lay(1, D_MODEL), lay(D_MODEL, IN_WIDTH),
                  lay(1, ATTN_WIDTH), lay(1, ATTN_WIDTH), lay(POOL_WIDTH, POOL_WIDTH), lay(1, POOL_WIDTH),
                  lay(LRU_CONV, LRU_WIDTH), lay(1, LRU_WIDTH), lay(LRU_WIDTH // LANES, LANES, 2 * LANES),
                  lay(LRU_WIDTH // LANES, 1, 2 * LANES), lay(1, LRU_WIDTH)],
        out_specs=[slab, slab, slab, tile(POOL_WIDTH), tile(LRU_WIDTH)],
        out_shape=[slab_shape, slab_shape, slab_shape,
                   jax.ShapeDtypeStruct((B, S, POOL_WIDTH), BF16), jax.ShapeDtypeStruct((B, S, LRU_WIDTH), BF16)],
        scratch_shapes=[pltpu.VMEM((POOL_HALO + tm, POOL_WIDTH), F32), pltpu.VMEM((SUBLANES + tm, LRU_WIDTH), F32),
                        pltpu.VMEM((SUBLANES, LRU_WIDTH), F32), pltpu.VMEM((tm, LRU_WIDTH), F32)],
        compiler_params=pltpu.CompilerParams(dimension_semantics=("arbitrary", "arbitrary"),
                                             vmem_limit_bytes=VMEM_LIMIT_BYTES),
        name="proj",
    )(h, p["norm_mix"], p["w_in"], p["q_gain"], p["k_gain"], p["pool_w"], p["pool_scale"],
      p["lru_conv_w"], p["lru_conv_b"], p["lru_wg"], p["lru_bg"], p["lru_lambda"])


def _attn_kernel(q_ref, k_ref, v_ref, bias_ref, o_ref, qs, ks, vs, oa, ma, la, ob, mb, lb):
    S = o_ref.shape[0]
    nblk = S // ATTN_BLOCK
    Q = ATTN_BLOCK
    low = lax.broadcasted_iota(jnp.int32, (Q, LANES), 1) < HEAD_DIM

    ks[0:Q, :] = jnp.zeros((Q, ATTN_WIDTH), BF16)
    vs[0:Q, :] = jnp.zeros((Q, ATTN_WIDTH), BF16)

    def regroup(src, d_from, step, store):
        Lf, Ln = S // d_from, S // (d_from * step)
        for rf in range(d_from):
            for rs in range(step):
                r_new = rs * d_from + rf
                for c in range(ATTN_PAIRS):
                    rows = pl.ds(rf * Lf + rs, Ln, stride=step) if step > 1 else slice(rf * Lf, (rf + 1) * Lf)
                    store(c, r_new * Ln, Ln, src[c, rows, :])

    def load_qkv(srcs, d_from, step):
        for src, dst, off in zip(srcs, (qs, ks, vs), (0, Q, Q)):
            def store(c, row, n, val, dst=dst, off=off):
                dst[off + row:off + row + n, c * LANES:(c + 1) * LANES] = val.astype(BF16)
            regroup(src, d_from, step, store)

    def reorder_f32(src, dst, d_from, step):
        def store(c, row, n, val):
            dst[c, row:row + n, :] = val
        regroup(src, d_from, step, store)

    def process(pidx, d, src, dst, d_next=None):
        nbl = (S // d) // Q

        def body(g, carry):
            r0 = pl.multiple_of(g * Q, Q)
            if nbl > 1:
                first = jnp.where((g & (nbl - 1)) == 0, 1, 0)
            if dst is not None:
                step = d // d_next
                r, jb = g >> (nbl.bit_length() - 1), g & (nbl - 1)
                out_rows = pl.ds((r & (d_next - 1)) * (S // d_next) + (r >> (d_next.bit_length() - 1))
                                 + jb * (step * Q), Q, stride=step)
            for c in range(ATTN_PAIRS):
                cols = slice(c * LANES, (c + 1) * LANES)
                qb = qs[pl.ds(r0, Q), cols]
                zero = jnp.zeros_like(qb)
                qq = jnp.concatenate([jnp.where(low, qb, zero), jnp.where(low, zero, qb)], axis=0)
                if nbl > 1:
                    kk = ks[pl.ds(r0, 2 * Q), cols]
                    vv = vs[pl.ds(r0, 2 * Q), cols]
                    sc = lax.dot_general(qq, kk, _NT, preferred_element_type=F32) + bias_ref[pidx, first, c]
                else:
                    kk = ks[pl.ds(r0 + Q, Q), cols]
                    vv = vs[pl.ds(r0 + Q, Q), cols]
                    sc = lax.dot_general(qq, kk, _NT, preferred_element_type=F32) + bias_ref[pidx, 1, c, :, Q:2 * Q]
                m = jnp.max(sc, axis=-1, keepdims=True)
                pr = jnp.exp2(sc - m)
                den = jnp.sum(pr, axis=-1, keepdims=True)
                pv = _dot(pr.astype(BF16), vv)
                o_new = jnp.where(low, pv[0:Q, :], pv[Q:2 * Q, :])
                m_new = jnp.where(low, m[0:Q, :], m[Q:2 * Q, :])
                l_new = jnp.where(low, den[0:Q, :], den[Q:2 * Q, :])
                if src is not None:
                    m_old = src[1][c, pl.ds(r0, Q), :]
                    mx = jnp.maximum(m_new, m_old)
                    wn = jnp.exp2(m_new - mx)
                    wo = jnp.exp2(m_old - mx)
                    o_new = wn * o_new + wo * src[0][c, pl.ds(r0, Q), :]
                    l_new = wn * l_new + wo * src[2][c, pl.ds(r0, Q), :]
                    m_new = mx
                if dst is None:
                    o_ref[pl.ds(r0, Q), cols] = (o_new / l_new).astype(BF16)
                else:
                    dst[0][c, out_rows, :] = o_new
                    dst[1][c, out_rows, :] = m_new
                    dst[2][c, out_rows, :] = l_new
            return carry

        lax.fori_loop(0, nblk, body, 0, unroll=ATTN_UNROLL if nbl > 1 else 2 * ATTN_UNROLL)

    d0, d1, d2 = DILATIONS
    acc_a, acc_b = (oa, ma, la), (ob, mb, lb)
    for src, dst in zip((q_ref, k_ref, v_ref), acc_b):
        reorder_f32(src, dst, d2, d1 // d2)
    load_qkv(acc_b, d1, d0 // d1)
    process(0, d0, None, acc_a, d1)
    load_qkv(acc_b, d1, 1)
    process(1, d1, acc_a, acc_b, d2)
    load_qkv((q_ref, k_ref, v_ref), d2, 1)
    process(2, d2, acc_b, None)


def _attn_bias():
    Q = ATTN_BLOCK
    dist = (Q + np.arange(Q)[:, None] - np.arange(2 * Q)[None, :])
    valid = (dist >= 0) & (dist <= Q)
    has_prev = np.arange(2 * Q)[None, :] >= Q
    slopes = np.asarray([2.0 ** (-8.0 * (h + 1) / ATTN_HEADS) for h in range(ATTN_HEADS)], dtype=np.float32)
    out = np.empty((len(DILATIONS), 2, ATTN_PAIRS, 2 * Q, 2 * Q), np.float32)
    for pi, d in enumerate(DILATIONS):
        for h in range(ATTN_HEADS):
            step = slopes[h] * np.float32(d)
            pen = (step * dist.astype(np.float32)).astype(np.float64) * LOG2E
            rows = slice((h % 2) * Q, (h % 2 + 1) * Q)
            out[pi, 0, h // 2, rows, :] = np.where(valid, -pen, NEG)
            out[pi, 1, h // 2, rows, :] = np.where(valid & has_prev, -pen, NEG)
    return jnp.asarray(out)


def _attn_call(q, k, v, bias):
    B, _, S, _ = q.shape
    slab = pl.BlockSpec((None, ATTN_PAIRS, S, LANES), lambda b: (b, 0, 0, 0))
    acc = pltpu.VMEM((ATTN_PAIRS, S, LANES), F32)
    return pl.pallas_call(
        _attn_kernel,
        grid=(B,),
        in_specs=[slab, slab, slab,
                  pl.BlockSpec(bias.shape, lambda b: (0, 0, 0, 0, 0), pipeline_mode=pl.Buffered(1))],
        out_specs=pl.BlockSpec((None, S, ATTN_WIDTH), lambda b: (b, 0, 0)),
        out_shape=jax.ShapeDtypeStruct((B, S, ATTN_WIDTH), BF16),
        scratch_shapes=[pltpu.VMEM((S, ATTN_WIDTH), BF16), pltpu.VMEM((S + ATTN_BLOCK, ATTN_WIDTH), BF16),
                        pltpu.VMEM((S + ATTN_BLOCK, ATTN_WIDTH), BF16), acc, acc, acc, acc, acc, acc],
        compiler_params=pltpu.CompilerParams(dimension_semantics=("arbitrary",),
                                             vmem_limit_bytes=VMEM_LIMIT_BYTES),
        name="attn",
    )(q, k, v, bias)


def _memkv_kernel(mem_ref, g_ref, w_ref, kg_ref, k_ref, v_ref):
    mn = _rms(mem_ref[...], g_ref[...]).astype(BF16)
    kv = _dot(mn, w_ref[...])
    for hh in range(MEM_HEADS):
        cols = slice(hh * MEM_HEAD_DIM, (hh + 1) * MEM_HEAD_DIM)
        k_ref[:, cols] = _rms(kv[:, cols], kg_ref[...]).astype(BF16)
    v_ref[...] = kv[:, D_MODEL:].astype(BF16)


def _memkv_call(mem, p):
    B, M, _ = mem.shape
    L = p["w_kv_mem"].shape[0]
    rows = B * M
    tm = min(TM_MEMKV, rows)
    lay = lambda *shape: pl.BlockSpec((None,) + shape, lambda l, t: (l,) + (0,) * len(shape))
    out = pl.BlockSpec((None, tm, D_MODEL), lambda l, t: (l, t, 0))
    shape = jax.ShapeDtypeStruct((L, rows, D_MODEL), BF16)
    k, v = pl.pallas_call(
        _memkv_kernel,
        grid=(L, rows // tm),
        in_specs=[pl.BlockSpec((tm, D_MODEL), lambda l, t: (t, 0)), lay(1, D_MODEL),
                  lay(D_MODEL, 2 * D_MODEL), lay(1, MEM_HEAD_DIM)],
        out_specs=[out, out],
        out_shape=[shape, shape],
        compiler_params=pltpu.CompilerParams(dimension_semantics=("arbitrary", "arbitrary"),
                                             vmem_limit_bytes=VMEM_LIMIT_BYTES),
        name="memkv",
    )(mem.reshape(rows, D_MODEL), p["norm_memkv"], p["w_kv_mem"], p["mk_gain"])
    return k.reshape(L, B, M, D_MODEL), v.reshape(L, B, M, D_MODEL)


def _mixmem_kernel(pool_ref, attn_ref, lru_ref, h_ref, wout_ref, g_ref, wq_ref, qg_ref, k_ref, v_ref, wo_ref,
                   o_ref, mix, obuf):
    mix[:, 0:POOL_WIDTH] = pool_ref[...]
    mix[:, POOL_WIDTH:POOL_WIDTH + ATTN_WIDTH] = attn_ref[...]
    mix[:, POOL_WIDTH + ATTN_WIDTH:] = lru_ref[...]
    h = h_ref[...] + _dot(mix[...], wout_ref[...])
    q = _dot(_rms(h, g_ref[...]).astype(BF16), wq_ref[...])
    for hh in range(MEM_HEADS):
        cols = slice(hh * MEM_HEAD_DIM, (hh + 1) * MEM_HEAD_DIM)
        qh = _rms(q[:, cols], qg_ref[...]).astype(BF16)
        sc = lax.dot_general(qh, k_ref[:, cols], _NT, preferred_element_type=F32)
        m = jnp.max(sc, axis=-1, keepdims=True)
        pr = jnp.exp(sc - m)
        pr = pr / jnp.sum(pr, axis=-1, keepdims=True)
        obuf[:, cols] = _dot(pr.astype(BF16), v_ref[:, cols]).astype(BF16)
    o_ref[...] = h + _dot(obuf[...], wo_ref[...])


def _mixmem_call(pool, attn, lru, h, kmem, vmem, l, p):
    B, S, _ = h.shape
    M = kmem.shape[2]
    tm = TM_MEM
    tile = lambda w: pl.BlockSpec((None, tm, w), lambda b, s: (b, s, 0))
    lay = lambda *shape: pl.BlockSpec((None,) + shape, lambda b, s: (l,) + (0,) * len(shape),
                                      pipeline_mode=pl.Buffered(1))
    kv = pl.BlockSpec((None, None, M, D_MODEL), lambda b, s: (l, b, 0, 0))
    return pl.pallas_call(
        _mixmem_kernel,
        grid=(B, S // tm),
        in_specs=[tile(POOL_WIDTH), tile(ATTN_WIDTH), tile(LRU_WIDTH), tile(D_MODEL), lay(D_MODEL, D_MODEL),
                  lay(1, D_MODEL), lay(D_MODEL, D_MODEL), lay(1, MEM_HEAD_DIM), kv, kv, lay(D_MODEL, D_MODEL)],
        out_specs=tile(D_MODEL),
        out_shape=jax.ShapeDtypeStruct(h.shape, F32),
        scratch_shapes=[pltpu.VMEM((tm, D_MODEL), BF16), pltpu.VMEM((tm, D_MODEL), BF16)],
        compiler_params=pltpu.CompilerParams(dimension_semantics=("arbitrary", "arbitrary"),
                                             vmem_limit_bytes=VMEM_LIMIT_BYTES),
        name="mixmem",
    )(pool, attn, lru, h, p["w_out"], p["norm_mem"], p["w_q_mem"], p["mq_gain"], kmem, vmem, p["w_o_mem"])


def _ffn_kernel(h_ref, g_ref, wup_ref, cw_ref, cb_ref, wdn_ref, o_ref, gcar, act):
    s = pl.program_id(1)
    tm = h_ref.shape[0]

    @pl.when(s == 0)
    def _():
        gcar[...] = jnp.zeros(gcar.shape, F32)

    h = h_ref[...]
    hn = _rms(h, g_ref[...]).astype(BF16)
    for c in range(D_FF // FFN_CHUNK):
        cols = slice(c * FFN_CHUNK, (c + 1) * FFN_CHUNK)
        g = _dot(hn, wup_ref[:, cols])
        u = _dot(hn, wup_ref[:, D_FF + c * FFN_CHUNK:D_FF + (c + 1) * FFN_CHUNK])
        ge = jnp.concatenate([gcar[:, cols], g], axis=0)
        conv = cb_ref[:, cols] + cw_ref[0:1, cols] * ge
        for kk in range(1, FFN_CONV):
            conv = conv + cw_ref[kk:kk + 1, cols] * pltpu.roll(ge, kk, 0)
        gcar[:, cols] = g[tm - SUBLANES:tm, :]
        act[:, cols] = (_gelu(conv[SUBLANES:, :]) * u).astype(BF16)
    o_ref[...] = h + _dot(act[...], wdn_ref[...])


def _ffn_call(h, l, p):
    B, S, _ = h.shape
    tm = TM_FFN
    tile = pl.BlockSpec((None, tm, D_MODEL), lambda b, s: (b, s, 0))
    lay = lambda *shape: pl.BlockSpec((None,) + shape, lambda b, s: (l,) + (0,) * len(shape),
                                      pipeline_mode=pl.Buffered(1))
    return pl.pallas_call(
        _ffn_kernel,
        grid=(B, S // tm),
        in_specs=[tile, lay(1, D_MODEL), lay(D_MODEL, 2 * D_FF), lay(FFN_CONV, D_FF), lay(1, D_FF),
                  lay(D_FF, D_MODEL)],
        out_specs=tile,
        out_shape=jax.ShapeDtypeStruct(h.shape, F32),
        scratch_shapes=[pltpu.VMEM((SUBLANES, D_FF), F32), pltpu.VMEM((tm, D_FF), BF16)],
        compiler_params=pltpu.CompilerParams(dimension_semantics=("arbitrary", "arbitrary"),
                                             vmem_limit_bytes=VMEM_LIMIT_BYTES),
        name="ffn",
    )(h, p["norm_ffn"], p["w_up"], p["ffn_conv_w"], p["ffn_conv_b"], p["w_down"])


def _block_diag(w):
    L, G, n, _ = w.shape
    eye = jnp.eye(G, dtype=w.dtype)
    return jnp.einsum("lgij,gh->lgihj", w, eye).reshape(L, G * n, G * n)


def _slab_diag(w):
    L, G, n, _ = w.shape
    per = LANES // n
    return _block_diag(w.reshape(L * (G // per), per, n, n)).reshape(L, G // per, LANES, LANES)


def kernel(x, mem, norm_mix, w_in, pool_w, pool_scale, q_gain, k_gain, lru_conv_w, lru_conv_b, lru_wa, lru_ba,
           lru_wx, lru_bx, lru_lambda, w_out, norm_mem, norm_memkv, w_q_mem, w_kv_mem, mq_gain, mk_gain, w_o_mem,
           norm_ffn, w_up, ffn_conv_w, ffn_conv_b, w_down):
    depth = w_in.shape[0]
    row = lambda a: a.astype(F32)[:, None, :]
    p = {
        "norm_mix": row(norm_mix),
        "w_in": jnp.concatenate([w_in[..., IN_WIDTH - 2 * LRU_WIDTH:], w_in[..., :IN_WIDTH - 2 * LRU_WIDTH]],
                                axis=-1).astype(BF16),
        "q_gain": row(jnp.tile(q_gain, (1, ATTN_HEADS)) * LOG2E),
        "k_gain": row(jnp.tile(k_gain, (1, ATTN_HEADS)) * (HEAD_DIM ** 0.5)),
        "pool_w": _block_diag(pool_w).astype(BF16),
        "pool_scale": row(pool_scale),
        "lru_conv_w": lru_conv_w.astype(F32),
        "lru_conv_b": row(lru_conv_b),
        "lru_wg": (0.5 * jnp.concatenate([_slab_diag(lru_wa), _slab_diag(lru_wx)], axis=-1)).astype(BF16),
        "lru_bg": 0.5 * jnp.concatenate([lru_ba.reshape(depth, -1, 1, LANES), lru_bx.reshape(depth, -1, 1, LANES)],
                                        axis=-1).astype(F32),
        "lru_lambda": row(lru_lambda),
        "w_out": w_out.astype(BF16),
        "norm_mem": row(norm_mem),
        "norm_memkv": row(norm_memkv),
        "w_q_mem": w_q_mem.astype(BF16),
        "w_kv_mem": w_kv_mem.astype(BF16),
        "mq_gain": row(mq_gain * (MEM_HEAD_DIM ** -0.5)),
        "mk_gain": row(mk_gain),
        "w_o_mem": w_o_mem.astype(BF16),
        "norm_ffn": row(norm_ffn),
        "w_up": w_up.astype(BF16),
        "ffn_conv_w": ffn_conv_w.astype(F32),
        "ffn_conv_b": row(ffn_conv_b),
        "w_down": w_down.astype(BF16),
    }
    bias = _attn_bias()
    kmem, vmem = _memkv_call(mem, p)
    h = x
    for l in range(depth):
        q, k, v, pool, lru = _proj_call(h, l, p)
        attn = _attn_call(q, k, v, bias)
        h = _mixmem_call(pool, attn, lru, h, kmem, vmem, l, p)
        h = _ffn_call(h, l, p)
    return h
```

```python
import numpy as np
import jax
import jax.numpy as jnp
from jax import lax
from jax.experimental import pallas as pl
from jax.experimental.pallas import tpu as pltpu

F32 = jnp.float32
BF16 = jnp.bfloat16

LANES = 128
SUBLANES = 8
VMEM_LIMIT_BYTES = 56 * 1024 * 1024

D_MODEL = 1024
POOL_WIDTH = 256
POOL_GROUPS = 4
POOL_WINDOWS = (2, 4, 8, 16)
POOL_HALO = 16
HEAD_DIM = 64
ATTN_WIDTH = 384
ATTN_HEADS = 6
ATTN_PAIRS = ATTN_WIDTH // LANES
DILATIONS = (16, 4, 1)
ATTN_BLOCK = 128
ATTN_STEP_PAIRS = 1
ATTN_UNROLL = 16
LRU_WIDTH = 384
LRU_BLOCKS = 6
LRU_CONV = 4
LRU_C = 8.0
IN_WIDTH = POOL_WIDTH + 3 * ATTN_WIDTH + 2 * LRU_WIDTH
MEM_HEADS = 4
MEM_HEAD_DIM = D_MODEL // MEM_HEADS
D_FF = 2816
FFN_CONV = 3
FFN_CHUNK = 256
EPS = 1e-6
F32_TINY = 1.1754944e-38
NEG = -1e30
LOG2E = 1.4426950408889634

TM_PROJ = 1024
TM_MEM = 1024
TM_FFN = 1024
TM_MEMKV = 1024

_NT = (((1,), (1,)), ((), ()))


def _rms(x, g):
    ms = jnp.mean(x * x, axis=-1, keepdims=True)
    return x * lax.rsqrt(ms + EPS) * g


def _gelu(x):
    c = 0.7978845608028654
    hx = 0.5 * x
    return hx + hx * jnp.tanh(x * (c + (c * 0.044715) * (x * x)))


def _dot(a, b):
    return jnp.dot(a, b, preferred_element_type=F32)


def _proj_kernel(h_ref, gmix_ref, win_ref, qg_ref, kg_ref, poolw_ref, pscale_ref,
                 cw_ref, cb_ref, wg_ref, bg_ref, lam_ref,
                 q_ref, k_ref, v_ref, pool_ref, lru_ref,
                 uext, xext, hcar, hbuf):
    s = pl.program_id(1)
    tm = h_ref.shape[0]

    @pl.when(s == 0)
    def _():
        uext[0:POOL_HALO, :] = jnp.zeros((POOL_HALO, POOL_WIDTH), F32)
        xext[0:SUBLANES, :] = jnp.zeros((SUBLANES, LRU_WIDTH), F32)
        hcar[...] = jnp.zeros(hcar.shape, F32)

    hn = _rms(h_ref[...], gmix_ref[...]).astype(BF16)
    proj = _dot(hn, win_ref[...])
    o = 2 * LRU_WIDTH
    x = proj[:, 0:LRU_WIDTH]
    y = proj[:, LRU_WIDTH:o]
    u = proj[:, o:o + POOL_WIDTH]
    q = proj[:, o + POOL_WIDTH:o + POOL_WIDTH + ATTN_WIDTH]
    k = proj[:, o + POOL_WIDTH + ATTN_WIDTH:o + POOL_WIDTH + 2 * ATTN_WIDTH]
    v = proj[:, o + POOL_WIDTH + 2 * ATTN_WIDTH:]

    xext[SUBLANES:SUBLANES + tm, :] = x
    ex = xext[...]
    cw = cw_ref[...]
    conv = cb_ref[...] + cw[0:1, :] * ex
    for kk in range(1, LRU_CONV):
        conv = conv + cw[kk:kk + 1, :] * pltpu.roll(ex, kk, 0)
    xc = conv[SUBLANES:, :]
    xext[0:SUBLANES, :] = x[tm - SUBLANES:tm, :]

    xcb = xc.astype(BF16)
    th_a, th_x = [], []
    for c in range(LRU_WIDTH // LANES):
        th = jnp.tanh(_dot(xcb[:, c * LANES:(c + 1) * LANES], wg_ref[c]) + bg_ref[c])
        th_a.append(th[:, 0:LANES])
        th_x.append(th[:, LANES:])
    ig = 0.5 * jnp.concatenate(th_x, axis=1) + 0.5
    z = -lam_ref[...]
    softplus = jnp.maximum(z, 0.0) + jnp.log1p(jnp.exp(-jnp.abs(z)))
    half = (-0.5 * LRU_C) * softplus
    log_a = half * jnp.concatenate(th_a, axis=1) + half
    a = jnp.exp(log_a)
    t = jnp.tanh(-log_a) * (1.0 + a * a)
    b = (t * lax.rsqrt(jnp.maximum(t, F32_TINY))) * (ig * xc)

    ngroups = tm // SUBLANES
    a = a.reshape(ngroups, SUBLANES, LRU_WIDTH)
    b = b.reshape(ngroups, SUBLANES, LRU_WIDTH)
    sub = lax.broadcasted_iota(jnp.int32, (ngroups, SUBLANES, LRU_WIDTH), 1)
    for sh in (1, 2, 4):
        inside = sub >= sh
        a_s = jnp.where(inside, pltpu.roll(a, sh, 1), 1.0)
        b_s = jnp.where(inside, pltpu.roll(b, sh, 1), 0.0)
        b = a * b_s + b
        a = a * a_s
    hp = hcar[0:1, :]
    for j in range(ngroups):
        rows = slice(j * SUBLANES, (j + 1) * SUBLANES)
        hj = a[j] * hp + b[j]
        hbuf[rows, :] = hj
        hp = hj[SUBLANES - 1:SUBLANES, :]
    hcar[...] = jnp.broadcast_to(hp, hcar.shape)
    lru_ref[...] = (hbuf[...] * _gelu(y)).astype(BF16)

    low_head = lax.broadcasted_iota(jnp.int32, (tm, LANES), 1) < HEAD_DIM

    def headnorm(t, gain):
        t2 = t * t
        parts = []
        for c in range(ATTN_PAIRS):
            blk = t2[:, c * LANES:(c + 1) * LANES]
            even = jnp.sum(jnp.where(low_head, blk, 0.0), axis=-1, keepdims=True)
            odd = jnp.sum(jnp.where(low_head, 0.0, blk), axis=-1, keepdims=True)
            parts.append(jnp.where(low_head, even, odd))
        ssq = jnp.concatenate(parts, axis=1)
        return t * lax.rsqrt(ssq + HEAD_DIM * EPS) * gain

    qn = headnorm(q, qg_ref[...])
    kn = headnorm(k, kg_ref[...])
    for c in range(ATTN_PAIRS):
        sl = slice(c * LANES, (c + 1) * LANES)
        q_ref[c] = qn[:, sl]
        k_ref[c] = kn[:, sl]
        v_ref[c] = v[:, sl]

    uext[POOL_HALO:POOL_HALO + tm, :] = u
    e = uext[...]
    s2 = e + pltpu.roll(e, 1, 0)
    s4 = s2 + pltpu.roll(s2, 2, 0)
    s8 = s4 + pltpu.roll(s4, 4, 0)
    s16 = s8 + pltpu.roll(s8, 8, 0)
    uext[0:POOL_HALO, :] = u[tm - POOL_HALO:tm, :]
    tp1 = (s * tm + 1 + lax.broadcasted_iota(jnp.int32, (tm, LANES), 0)).astype(F32)
    low = lax.broadcasted_iota(jnp.int32, (tm, LANES), 1) < POOL_WIDTH // POOL_GROUPS

    def wmean(sw, col, w):
        return sw[POOL_HALO:, col * LANES:(col + 1) * LANES] / jnp.minimum(tp1, float(w))

    pooled = jnp.concatenate(
        [jnp.where(low, wmean(s2, 0, POOL_WINDOWS[0]), wmean(s4, 0, POOL_WINDOWS[1])) - u[:, 0:LANES],
         jnp.where(low, wmean(s8, 1, POOL_WINDOWS[2]), wmean(s16, 1, POOL_WINDOWS[3])) - u[:, LANES:2 * LANES]],
        axis=1)
    pool_ref[...] = (_dot(pooled.astype(BF16), poolw_ref[...]) * pscale_ref[...]).astype(BF16)


def _proj_call(h, l, p):
    B, S, _ = h.shape
    tm = TM_PROJ
    grid = (B, S // tm)
    tile = lambda w: pl.BlockSpec((None, tm, w), lambda b, s: (b, s, 0))
    slab = pl.BlockSpec((None, ATTN_PAIRS, tm, LANES), lambda b, s: (b, 0, s, 0))
    lay = lambda *shape: pl.BlockSpec((None,) + shape, lambda b, s: (l,) + (0,) * len(shape))
    slab_shape = jax.ShapeDtypeStruct((B, ATTN_PAIRS, S, LANES), F32)
    return pl.pallas_call(
        _proj_kernel,
        grid=grid,
        in_specs=[tile(D_MODEL), lay(1, D_MODEL), lay(D_MODEL, IN_WIDTH),
                  lay(1, ATTN_WIDTH), lay(1, ATTN_WIDTH), lay(POOL_WIDTH, POOL_WIDTH), lay(1, POOL_WIDTH),
                  lay(LRU_CONV, LRU_WIDTH), lay(1, LRU_WIDTH), lay(LRU_WIDTH // LANES, LANES, 2 * LANES),
                  lay(LRU_WIDTH // LANES, 1, 2 * LANES), lay(1, LRU_WIDTH)],
        out_specs=[slab, slab, slab, tile(POOL_WIDTH), tile(LRU_WIDTH)],
        out_shape=[slab_shape, slab_shape, slab_shape,
                   jax.ShapeDtypeStruct((B, S, POOL_WIDTH), BF16), jax.ShapeDtypeStruct((B, S, LRU_WIDTH), BF16)],
        scratch_shapes=[pltpu.VMEM((POOL_HALO + tm, POOL_WIDTH), F32), pltpu.VMEM((SUBLANES + tm, LRU_WIDTH), F32),
                        pltpu.VMEM((SUBLANES, LRU_WIDTH), F32), pltpu.VMEM((tm, LRU_WIDTH), F32)],
        compiler_params=pltpu.CompilerParams(dimension_semantics=("arbitrary", "arbitrary"),
                                             vmem_limit_bytes=VMEM_LIMIT_BYTES),
        name="proj",
    )(h, p["norm_mix"], p["w_in"], p["q_gain"], p["k_gain"], p["pool_w"], p["pool_scale"],
      p["lru_conv_w"], p["lru_conv_b"], p["lru_wg"], p["lru_bg"], p["lru_lambda"])


def _attn_kernel(q_ref, k_ref, v_ref, bias_ref, o_ref, qs, ks, vs, oa, ma, la, ob, mb, lb):
    S = o_ref.shape[0]
    npairs = q_ref.shape[0]
    nblk = S // ATTN_BLOCK
    Q = ATTN_BLOCK
    low = lax.broadcasted_iota(jnp.int32, (Q, LANES), 1) < HEAD_DIM

    ks[0:Q, :] = jnp.zeros((Q, ks.shape[1]), BF16)
    vs[0:Q, :] = jnp.zeros((Q, vs.shape[1]), BF16)

    def regroup(src, d_from, step, store):
        Lf, Ln = S // d_from, S // (d_from * step)
        for rf in range(d_from):
            for rs in range(step):
                r_new = rs * d_from + rf
                for c in range(npairs):
                    rows = pl.ds(rf * Lf + rs, Ln, stride=step) if step > 1 else slice(rf * Lf, (rf + 1) * Lf)
                    store(c, r_new * Ln, Ln, src[c, rows, :])

    def load_qkv(srcs, d_from, step):
        for src, dst, off in zip(srcs, (qs, ks, vs), (0, Q, Q)):
            def store(c, row, n, val, dst=dst, off=off):
                dst[off + row:off + row + n, c * LANES:(c + 1) * LANES] = val.astype(BF16)
            regroup(src, d_from, step, store)

    def reorder_f32(src, dst, d_from, step):
        def store(c, row, n, val):
            dst[c, row:row + n, :] = val
        regroup(src, d_from, step, store)

    def process(pidx, d, src, dst, d_next=None):
        nbl = (S // d) // Q

        def body(g, carry):
            r0 = pl.multiple_of(g * Q, Q)
            if nbl > 1:
                first = jnp.where((g & (nbl - 1)) == 0, 1, 0)
            if dst is not None:
                step = d // d_next
                r, jb = g >> (nbl.bit_length() - 1), g & (nbl - 1)
                out_rows = pl.ds((r & (d_next - 1)) * (S // d_next) + (r >> (d_next.bit_length() - 1))
                                 + jb * (step * Q), Q, stride=step)
            for c in range(npairs):
                cols = slice(c * LANES, (c + 1) * LANES)
                qb = qs[pl.ds(r0, Q), cols]
                zero = jnp.zeros_like(qb)
                qq = jnp.concatenate([jnp.where(low, qb, zero), jnp.where(low, zero, qb)], axis=0)
                if nbl > 1:
                    kk = ks[pl.ds(r0, 2 * Q), cols]
                    vv = vs[pl.ds(r0, 2 * Q), cols]
                    sc = lax.dot_general(qq, kk, _NT, preferred_element_type=F32) + bias_ref[pidx, first, c]
                else:
                    kk = ks[pl.ds(r0 + Q, Q), cols]
                    vv = vs[pl.ds(r0 + Q, Q), cols]
                    sc = lax.dot_general(qq, kk, _NT, preferred_element_type=F32) + bias_ref[pidx, 1, c, :, Q:2 * Q]
                m = jnp.max(sc, axis=-1, keepdims=True)
                pr = jnp.exp2(sc - m)
                den = jnp.sum(pr, axis=-1, keepdims=True)
                pv = _dot(pr.astype(BF16), vv)
                o_new = jnp.where(low, pv[0:Q, :], pv[Q:2 * Q, :])
                m_new = jnp.where(low, m[0:Q, :], m[Q:2 * Q, :])
                l_new = jnp.where(low, den[0:Q, :], den[Q:2 * Q, :])
                if src is not None:
                    m_old = src[1][c, pl.ds(r0, Q), :]
                    mx = jnp.maximum(m_new, m_old)
                    wn = jnp.exp2(m_new - mx)
                    wo = jnp.exp2(m_old - mx)
                    o_new = wn * o_new + wo * src[0][c, pl.ds(r0, Q), :]
                    l_new = wn * l_new + wo * src[2][c, pl.ds(r0, Q), :]
                    m_new = mx
                if dst is None:
                    o_ref[pl.ds(r0, Q), cols] = (o_new / l_new).astype(BF16)
                else:
                    dst[0][c, out_rows, :] = o_new
                    dst[1][c, out_rows, :] = m_new
                    dst[2][c, out_rows, :] = l_new
            return carry

        lax.fori_loop(0, nblk, body, 0, unroll=ATTN_UNROLL)

    d0, d1, d2 = DILATIONS
    acc_a, acc_b = (oa, ma, la), (ob, mb, lb)
    for src, dst in zip((q_ref, k_ref, v_ref), acc_b):
        reorder_f32(src, dst, d2, d1 // d2)
    load_qkv(acc_b, d1, d0 // d1)
    process(0, d0, None, acc_a, d1)
    load_qkv(acc_b, d1, 1)
    process(1, d1, acc_a, acc_b, d2)
    load_qkv((q_ref, k_ref, v_ref), d2, 1)
    process(2, d2, acc_b, None)


def _attn_bias():
    Q = ATTN_BLOCK
    dist = (Q + np.arange(Q)[:, None] - np.arange(2 * Q)[None, :])
    valid = (dist >= 0) & (dist <= Q)
    has_prev = np.arange(2 * Q)[None, :] >= Q
    slopes = np.asarray([2.0 ** (-8.0 * (h + 1) / ATTN_HEADS) for h in range(ATTN_HEADS)], dtype=np.float32)
    out = np.empty((len(DILATIONS), 2, ATTN_PAIRS, 2 * Q, 2 * Q), np.float32)
    for pi, d in enumerate(DILATIONS):
        for h in range(ATTN_HEADS):
            step = slopes[h] * np.float32(d)
            pen = (step * dist.astype(np.float32)).astype(np.float64) * LOG2E
            rows = slice((h % 2) * Q, (h % 2 + 1) * Q)
            out[pi, 0, h // 2, rows, :] = np.where(valid, -pen, NEG)
            out[pi, 1, h // 2, rows, :] = np.where(valid & has_prev, -pen, NEG)
    return jnp.asarray(out)


def _attn_call(q, k, v, bias):
    B, _, S, _ = q.shape
    n = ATTN_STEP_PAIRS
    width = n * LANES
    slab = pl.BlockSpec((None, n, S, LANES), lambda b, c: (b, c, 0, 0))
    acc = pltpu.VMEM((n, S, LANES), F32)
    nd = bias.shape[0]
    return pl.pallas_call(
        _attn_kernel,
        grid=(B, ATTN_PAIRS // n),
        in_specs=[slab, slab, slab,
                  pl.BlockSpec((nd, 2, n, 2 * ATTN_BLOCK, 2 * ATTN_BLOCK), lambda b, c: (0, 0, c, 0, 0))],
        out_specs=pl.BlockSpec((None, S, width), lambda b, c: (b, 0, c)),
        out_shape=jax.ShapeDtypeStruct((B, S, ATTN_WIDTH), BF16),
        scratch_shapes=[pltpu.VMEM((S, width), BF16), pltpu.VMEM((S + ATTN_BLOCK, width), BF16),
                        pltpu.VMEM((S + ATTN_BLOCK, width), BF16), acc, acc, acc, acc, acc, acc],
        compiler_params=pltpu.CompilerParams(dimension_semantics=("arbitrary", "arbitrary"),
                                             vmem_limit_bytes=VMEM_LIMIT_BYTES),
        name="attn",
    )(q, k, v, bias)


def _memkv_kernel(mem_ref, g_ref, w_ref, kg_ref, k_ref, v_ref):
    mn = _rms(mem_ref[...], g_ref[...]).astype(BF16)
    kv = _dot(mn, w_ref[...])
    for hh in range(MEM_HEADS):
        cols = slice(hh * MEM_HEAD_DIM, (hh + 1) * MEM_HEAD_DIM)
        k_ref[:, cols] = _rms(kv[:, cols], kg_ref[...]).astype(BF16)
    v_ref[...] = kv[:, D_MODEL:].astype(BF16)


def _memkv_call(mem, p):
    B, M, _ = mem.shape
    L = p["w_kv_mem"].shape[0]
    rows = B * M
    tm = min(TM_MEMKV, rows)
    lay = lambda *shape: pl.BlockSpec((None,) + shape, lambda l, t: (l,) + (0,) * len(shape))
    out = pl.BlockSpec((None, tm, D_MODEL), lambda l, t: (l, t, 0))
    shape = jax.ShapeDtypeStruct((L, rows, D_MODEL), BF16)
    k, v = pl.pallas_call(
        _memkv_kernel,
        grid=(L, rows // tm),
        in_specs=[pl.BlockSpec((tm, D_MODEL), lambda l, t: (t, 0)), lay(1, D_MODEL),
                  lay(D_MODEL, 2 * D_MODEL), lay(1, MEM_HEAD_DIM)],
        out_specs=[out, out],
        out_shape=[shape, shape],
        compiler_params=pltpu.CompilerParams(dimension_semantics=("arbitrary", "arbitrary"),
                                             vmem_limit_bytes=VMEM_LIMIT_BYTES),
        name="memkv",
    )(mem.reshape(rows, D_MODEL), p["norm_memkv"], p["w_kv_mem"], p["mk_gain"])
    return k.reshape(L, B, M, D_MODEL), v.reshape(L, B, M, D_MODEL)


def _mixmem_kernel(pool_ref, attn_ref, lru_ref, h_ref, wout_ref, g_ref, wq_ref, qg_ref, k_ref, v_ref, wo_ref,
                   o_ref, mix, obuf):
    mix[:, 0:POOL_WIDTH] = pool_ref[...]
    mix[:, POOL_WIDTH:POOL_WIDTH + ATTN_WIDTH] = attn_ref[...]
    mix[:, POOL_WIDTH + ATTN_WIDTH:] = lru_ref[...]
    h = h_ref[...] + _dot(mix[...], wout_ref[...])
    q = _dot(_rms(h, g_ref[...]).astype(BF16), wq_ref[...])
    for hh in range(MEM_HEADS):
        cols = slice(hh * MEM_HEAD_DIM, (hh + 1) * MEM_HEAD_DIM)
        qh = _rms(q[:, cols], qg_ref[...]).astype(BF16)
        sc = lax.dot_general(qh, k_ref[:, cols], _NT, preferred_element_type=F32)
        m = jnp.max(sc, axis=-1, keepdims=True)
        pr = jnp.exp(sc - m)
        pr = pr / jnp.sum(pr, axis=-1, keepdims=True)
        obuf[:, cols] = _dot(pr.astype(BF16), v_ref[:, cols]).astype(BF16)
    o_ref[...] = h + _dot(obuf[...], wo_ref[...])


def _mixmem_call(pool, attn, lru, h, kmem, vmem, l, p):
    B, S, _ = h.shape
    M = kmem.shape[2]
    tm = TM_MEM
    tile = lambda w: pl.BlockSpec((None, tm, w), lambda b, s: (b, s, 0))
    lay = lambda *shape: pl.BlockSpec((None,) + shape, lambda b, s: (l,) + (0,) * len(shape),
                                      pipeline_mode=pl.Buffered(1))
    kv = pl.BlockSpec((None, None, M, D_MODEL), lambda b, s: (l, b, 0, 0))
    return pl.pallas_call(
        _mixmem_kernel,
        grid=(B, S // tm),
        in_specs=[tile(POOL_WIDTH), tile(ATTN_WIDTH), tile(LRU_WIDTH), tile(D_MODEL), lay(D_MODEL, D_MODEL),
                  lay(1, D_MODEL), lay(D_MODEL, D_MODEL), lay(1, MEM_HEAD_DIM), kv, kv, lay(D_MODEL, D_MODEL)],
        out_specs=tile(D_MODEL),
        out_shape=jax.ShapeDtypeStruct(h.shape, F32),
        scratch_shapes=[pltpu.VMEM((tm, D_MODEL), BF16), pltpu.VMEM((tm, D_MODEL), BF16)],
        compiler_params=pltpu.CompilerParams(dimension_semantics=("arbitrary", "arbitrary"),
                                             vmem_limit_bytes=VMEM_LIMIT_BYTES),
        name="mixmem",
    )(pool, attn, lru, h, p["w_out"], p["norm_mem"], p["w_q_mem"], p["mq_gain"], kmem, vmem, p["w_o_mem"])


def _ffn_kernel(h_ref, g_ref, wup_ref, cw_ref, cb_ref, wdn_ref, o_ref, gcar, act):
    s = pl.program_id(1)
    tm = h_ref.shape[0]

    @pl.when(s == 0)
    def _():
        gcar[...] = jnp.zeros(gcar.shape, F32)

    h = h_ref[...]
    hn = _rms(h, g_ref[...]).astype(BF16)
    for c in range(D_FF // FFN_CHUNK):
        cols = slice(c * FFN_CHUNK, (c + 1) * FFN_CHUNK)
        g = _dot(hn, wup_ref[:, cols])
        u = _dot(hn, wup_ref[:, D_FF + c * FFN_CHUNK:D_FF + (c + 1) * FFN_CHUNK])
        ge = jnp.concatenate([gcar[:, cols], g], axis=0)
        conv = cb_ref[:, cols] + cw_ref[0:1, cols] * ge
        for kk in range(1, FFN_CONV):
            conv = conv + cw_ref[kk:kk + 1, cols] * pltpu.roll(ge, kk, 0)
        gcar[:, cols] = g[tm - SUBLANES:tm, :]
        act[:, cols] = (_gelu(conv[SUBLANES:, :]) * u).astype(BF16)
    o_ref[...] = h + _dot(act[...], wdn_ref[...])


def _ffn_call(h, l, p):
    B, S, _ = h.shape
    tm = TM_FFN
    tile = pl.BlockSpec((None, tm, D_MODEL), lambda b, s: (b, s, 0))
    lay = lambda *shape: pl.BlockSpec((None,) + shape, lambda b, s: (l,) + (0,) * len(shape),
                                      pipeline_mode=pl.Buffered(1))
    return pl.pallas_call(
        _ffn_kernel,
        grid=(B, S // tm),
        in_specs=[tile, lay(1, D_MODEL), lay(D_MODEL, 2 * D_FF), lay(FFN_CONV, D_FF), lay(1, D_FF),
                  lay(D_FF, D_MODEL)],
        out_specs=tile,
        out_shape=jax.ShapeDtypeStruct(h.shape, F32),
        scratch_shapes=[pltpu.VMEM((SUBLANES, D_FF), F32), pltpu.VMEM((tm, D_FF), BF16)],
        compiler_params=pltpu.CompilerParams(dimension_semantics=("arbitrary", "arbitrary"),
                                             vmem_limit_bytes=VMEM_LIMIT_BYTES),
        name="ffn",
    )(h, p["norm_ffn"], p["w_up"], p["ffn_conv_w"], p["ffn_conv_b"], p["w_down"])


def _block_diag(w):
    L, G, n, _ = w.shape
    eye = jnp.eye(G, dtype=w.dtype)
    return jnp.einsum("lgij,gh->lgihj", w, eye).reshape(L, G * n, G * n)


def _slab_diag(w):
    L, G, n, _ = w.shape
    per = LANES // n
    return _block_diag(w.reshape(L * (G // per), per, n, n)).reshape(L, G // per, LANES, LANES)


def kernel(x, mem, norm_mix, w_in, pool_w, pool_scale, q_gain, k_gain, lru_conv_w, lru_conv_b, lru_wa, lru_ba,
           lru_wx, lru_bx, lru_lambda, w_out, norm_mem, norm_memkv, w_q_mem, w_kv_mem, mq_gain, mk_gain, w_o_mem,
           norm_ffn, w_up, ffn_conv_w, ffn_conv_b, w_down):
    depth = w_in.shape[0]
    row = lambda a: a.astype(F32)[:, None, :]
    p = {
        "norm_mix": row(norm_mix),
        "w_in": jnp.concatenate([w_in[..., IN_WIDTH - 2 * LRU_WIDTH:], w_in[..., :IN_WIDTH - 2 * LRU_WIDTH]],
                                axis=-1).astype(BF16),
        "q_gain": row(jnp.tile(q_gain, (1, ATTN_HEADS)) * LOG2E),
        "k_gain": row(jnp.tile(k_gain, (1, ATTN_HEADS)) * (HEAD_DIM ** 0.5)),
        "pool_w": _block_diag(pool_w).astype(BF16),
        "pool_scale": row(pool_scale),
        "lru_conv_w": lru_conv_w.astype(F32),
        "lru_conv_b": row(lru_conv_b),
        "lru_wg": (0.5 * jnp.concatenate([_slab_diag(lru_wa), _slab_diag(lru_wx)], axis=-1)).astype(BF16),
        "lru_bg": 0.5 * jnp.concatenate([lru_ba.reshape(depth, -1, 1, LANES), lru_bx.reshape(depth, -1, 1, LANES)],
                                        axis=-1).astype(F32),
        "lru_lambda": row(lru_lambda),
        "w_out": w_out.astype(BF16),
        "norm_mem": row(norm_mem),
        "norm_memkv": row(norm_memkv),
        "w_q_mem": w_q_mem.astype(BF16),
        "w_kv_mem": w_kv_mem.astype(BF16),
        "mq_gain": row(mq_gain * (MEM_HEAD_DIM ** -0.5)),
        "mk_gain": row(mk_gain),
        "w_o_mem": w_o_mem.astype(BF16),
        "norm_ffn": row(norm_ffn),
        "w_up": w_up.astype(BF16),
        "ffn_conv_w": ffn_conv_w.astype(F32),
        "ffn_conv_b": row(ffn_conv_b),
        "w_down": w_down.astype(BF16),
    }
    bias = _attn_bias()
    kmem, vmem = _memkv_call(mem, p)
    h = x
    for l in range(depth):
        q, k, v, pool, lru = _proj_call(h, l, p)
        attn = _attn_call(q, k, v, bias)
        h = _mixmem_call(pool, attn, lru, h, kmem, vmem, l, p)
        h = _ffn_call(h, l, p)
    return h
```
